```python
import jax, jax.numpy as jnp
from jax import lax
import numpy as np

D_MODEL = 1024
BATCH = 2
SEQ = 8192
DEPTH = 2

A_HEADS = 8
HEAD_DIM = 64
A_WIDTH = A_HEADS * HEAD_DIM
IDX_HEADS = 4
IDX_DIM = 64
TOPK_MAX = 256
Q_BLOCK = 128
ATTN_SCALE = HEAD_DIM ** -0.5
IDX_SCALE = (IDX_DIM ** -0.5) * (IDX_HEADS ** -0.5)

ROPE_THETA = 500000.0
ROT_DIM = HEAD_DIM // 4

B_HEADS = 4
B_DK = 32
B_DV = 64
B_KWIDTH = B_HEADS * B_DK
B_VWIDTH = B_HEADS * B_DV
GATE_RANK = 16
GATE_TAU = 16.0
CHUNK = 64

POOL_WINDOWS = (2, 4, 8, 16)
C_GROUPS = 4
C_GROUP_DIM = 64
C_WIDTH = C_GROUPS * C_GROUP_DIM

MIX_WIDTH = A_WIDTH + B_VWIDTH + C_WIDTH

SPLIT_SIZES = (
    A_WIDTH, A_WIDTH, A_WIDTH,
    IDX_HEADS * IDX_DIM, IDX_DIM, IDX_HEADS,
    B_KWIDTH, B_KWIDTH, B_VWIDTH, B_VWIDTH,
    GATE_RANK,
    C_WIDTH,
)
IN_WIDTH = sum(SPLIT_SIZES)

D_FF = 2816
CONV_WIDTH = 3
EPS = 1e-6

kernel_name = "hymba_dsa_gla_pool_convglu"


def rms_norm(x, g):
    xf = x.astype(jnp.float32)
    y = xf * lax.rsqrt(jnp.mean(xf * xf, axis=-1, keepdims=True) + EPS)
    return (y * g.astype(jnp.float32)).astype(x.dtype)


def rope_tables(positions):
    inv = ROPE_THETA ** (-jnp.arange(0, ROT_DIM, 2, dtype=jnp.float32) / ROT_DIM)
    ang = positions.astype(jnp.float32)[..., None] * inv
    return jnp.cos(ang), jnp.sin(ang)


def partial_rope(x, cos, sin):
    half = ROT_DIM // 2
    x1 = x[..., :half].astype(jnp.float32)
    x2 = x[..., half:ROT_DIM].astype(jnp.float32)
    c = cos[:, :, None, :]
    s = sin[:, :, None, :]
    rot = jnp.concatenate([x1 * c - x2 * s, x2 * c + x1 * s], axis=-1).astype(x.dtype)
    return jnp.concatenate([rot, x[..., ROT_DIM:]], axis=-1)


def split_cols(p):
    outs = []
    off = 0
    for n in SPLIT_SIZES:
        outs.append(p[..., off:off + n])
        off += n
    return outs


def dsa_attention(q, k, v, qi, ki, wi):
    nbat, S, H, Dh = q.shape
    n_keep = min(TOPK_MAX, S // 4)
    nb = S // Q_BLOCK
    key_pos = jnp.arange(S)
    ki_f = ki.astype(jnp.float32)

    def to_blocks(a):
        return a.reshape(nbat, nb, Q_BLOCK, *a.shape[2:]).swapaxes(0, 1)

    def block(args):
        qb, qib, wib, qpos = args
        logits = jnp.einsum('bqhd,bsd->bqhs', qib.astype(jnp.float32), ki_f)
        score = jnp.einsum('bqh,bqhs->bqs', wib.astype(jnp.float32), jax.nn.relu(logits)) * IDX_SCALE
        causal = key_pos[None, :] <= qpos[:, None]
        score = jnp.where(causal[None], score, -jnp.inf)
        _, idx = lax.top_k(score, n_keep)
        valid = idx <= qpos[None, :, None]
        kg = jax.vmap(lambda kk, ii: kk[ii])(k, idx)
        vg = jax.vmap(lambda vv, ii: vv[ii])(v, idx)
        s = jnp.einsum('bqhd,bqkhd->bhqk', qb.astype(jnp.float32), kg.astype(jnp.float32)) * ATTN_SCALE
        s = jnp.where(valid[:, None], s, -jnp.inf)
        p = jax.nn.softmax(s, axis=-1)
        o = jnp.einsum('bhqk,bqkhd->bqhd', p, vg.astype(jnp.float32))
        return o.astype(q.dtype)

    qpos_blocks = jnp.arange(S).reshape(nb, Q_BLOCK)
    out = lax.map(block, (to_blocks(q), to_blocks(qi), to_blocks(wi), qpos_blocks))
    return out.swapaxes(0, 1).reshape(nbat, S, H, Dh)


def gla_chunked(q, k, v, log_a):
    nbat, S, H, DK = q.shape
    DV = v.shape[-1]
    nc = S // CHUNK

    def chunks(a):
        return a.astype(jnp.float32).reshape(nbat, nc, CHUNK, H, a.shape[-1]).transpose(1, 0, 3, 2, 4)

    qc = chunks(q * (DK ** -0.5))
    kc, vc, gc = chunks(k), chunks(v), chunks(log_a)
    tri = jnp.tril(jnp.ones((CHUNK, CHUNK), dtype=bool))

    def step(state, inp):
        qq, kk, vv, gg = inp
        b = jnp.cumsum(gg, axis=2)
        o_inter = jnp.einsum('bhcd,bhde->bhce', qq * jnp.exp(b), state)
        diff = b[:, :, :, None, :] - b[:, :, None, :, :]
        decay = jnp.exp(jnp.where(tri[:, :, None], diff, -jnp.inf))
        att = jnp.einsum('bhid,bhjd,bhijd->bhij', qq, kk, decay)
        o_intra = jnp.einsum('bhij,bhje->bhie', att, vv)
        b_last = b[:, :, -1:, :]
        new_state = state * jnp.exp(b_last[:, :, 0, :, None]) + jnp.einsum(
            'bhcd,bhce->bhde', kk * jnp.exp(b_last - b), vv)
        return new_state, o_inter + o_intra

    state0 = jnp.zeros((nbat, H, DK, DV), jnp.float32)
    _, o = lax.scan(step, state0, (qc, kc, vc, gc))
    return o.transpose(1, 0, 3, 2, 4).reshape(nbat, S, H, DV)


def multiscale_pool(u, w_pool, pool_scale):
    nbat, S, _ = u.shape
    uf = u.astype(jnp.float32).reshape(nbat, S, C_GROUPS, C_GROUP_DIM)
    cs = jnp.concatenate([jnp.zeros((nbat, 1, C_GROUPS, C_GROUP_DIM), jnp.float32),
                          jnp.cumsum(uf, axis=1)], axis=1)
    pos = jnp.arange(S)
    win = jnp.array(POOL_WINDOWS, dtype=jnp.int32)
    lo_idx = jnp.maximum(pos[:, None] + 1 - win[None, :], 0)
    lo = cs[:, lo_idx, jnp.arange(C_GROUPS)[None, :]]
    cnt = jnp.minimum(pos[:, None] + 1, win[None, :]).astype(jnp.float32)
    pooled = (cs[:, 1:] - lo) / cnt[None, :, :, None] - uf
    y = jnp.einsum('bsgd,gde->bsge', pooled, w_pool.astype(jnp.float32))
    y = y * pool_scale.astype(jnp.float32).reshape(C_GROUPS, C_GROUP_DIM)
    return y.reshape(nbat, S, C_WIDTH).astype(u.dtype)


def conv_glu_ffn(x, w_up, conv_w, conv_b, w_down):
    S = x.shape[1]
    h = x @ w_up
    a, b = h[..., :D_FF], h[..., D_FF:]
    ap = jnp.pad(a, ((0, 0), (CONV_WIDTH - 1, 0), (0, 0)))
    conv = conv_b + ap[:, 0:S] * conv_w[0]
    for j in range(1, CONV_WIDTH):
        conv = conv + ap[:, j:j + S] * conv_w[j]
    return (jax.nn.silu(conv) * b) @ w_down


def hybrid_layer(x, cos, sin, norm1_g, w_in, w_gate_up, b_gate, gla_norm_g, w_pool, pool_scale,
                 w_out, norm2_g, w_up, conv_w, conv_b, w_down):
    nbat, S, _ = x.shape
    h = rms_norm(x, norm1_g)
    proj = h @ w_in
    (qa, ka, va, qi, ki, wi, qb, kb, vb, rb, gl, uc) = split_cols(proj)

    qa = partial_rope(qa.reshape(nbat, S, A_HEADS, HEAD_DIM), cos, sin)
    ka = partial_rope(ka.reshape(nbat, S, A_HEADS, HEAD_DIM), cos, sin)
    va = va.reshape(nbat, S, A_HEADS, HEAD_DIM)
    qi = partial_rope(qi.reshape(nbat, S, IDX_HEADS, IDX_DIM), cos, sin)
    ki = partial_rope(ki[:, :, None, :], cos, sin)[:, :, 0, :]
    o_a = dsa_attention(qa, ka, va, qi, ki, wi).reshape(nbat, S, A_WIDTH)

    z = (gl @ w_gate_up + b_gate).astype(jnp.float32)
    log_a = (jax.nn.log_sigmoid(z) / GATE_TAU).reshape(nbat, S, B_HEADS, B_DK)
    o_b = gla_chunked(qb.reshape(nbat, S, B_HEADS, B_DK), kb.reshape(nbat, S, B_HEADS, B_DK),
                      vb.reshape(nbat, S, B_HEADS, B_DV), log_a)
    o_b = rms_norm(o_b, gla_norm_g).reshape(nbat, S, B_VWIDTH)
    o_b = (o_b * jax.nn.silu(rb.astype(jnp.float32))).astype(x.dtype)

    o_c = multiscale_pool(uc, w_pool, pool_scale)

    mix = jnp.concatenate([o_a.astype(x.dtype), o_b, o_c], axis=-1) @ w_out
    x = x + mix
    x = x + conv_glu_ffn(rms_norm(x, norm2_g), w_up, conv_w, conv_b, w_down)
    return x


def setup_inputs(seed: int = 0) -> dict:
    key = jax.random.key(seed)
    ks = jax.random.split(key, 16)
    f32 = jnp.float32
    nrm = lambda k, shape, s: jax.random.normal(k, shape, f32) * s
    return {
        "x": jax.random.normal(ks[0], (BATCH, SEQ, D_MODEL), f32),
        "positions": jnp.broadcast_to(jnp.arange(SEQ, dtype=jnp.int32)[None, :], (BATCH, SEQ)),
        "norm1_g": 1.0 + nrm(ks[1], (DEPTH, D_MODEL), 0.02),
        "w_in": nrm(ks[2], (DEPTH, D_MODEL, IN_WIDTH), D_MODEL ** -0.5),
        "w_gate_up": nrm(ks[3], (DEPTH, GATE_RANK, B_KWIDTH), GATE_RANK ** -0.5),
        "b_gate": nrm(ks[4], (DEPTH, B_KWIDTH), 0.01),
        "gla_norm_g": 1.0 + nrm(ks[5], (DEPTH, B_HEADS, B_DV), 0.02),
        "w_pool": nrm(ks[6], (DEPTH, C_GROUPS, C_GROUP_DIM, C_GROUP_DIM), C_GROUP_DIM ** -0.5),
        "pool_scale": 1.0 + nrm(ks[7], (DEPTH, C_WIDTH), 0.1),
        "w_out": nrm(ks[8], (DEPTH, MIX_WIDTH, D_MODEL), MIX_WIDTH ** -0.5),
        "norm2_g": 1.0 + nrm(ks[9], (DEPTH, D_MODEL), 0.02),
        "w_up": nrm(ks[10], (DEPTH, D_MODEL, 2 * D_FF), D_MODEL ** -0.5),
        "conv_w": nrm(ks[11], (DEPTH, CONV_WIDTH, D_FF), CONV_WIDTH ** -0.5),
        "conv_b": nrm(ks[12], (DEPTH, D_FF), 0.01),
        "w_down": nrm(ks[13], (DEPTH, D_FF, D_MODEL), D_FF ** -0.5),
        "final_norm_g": 1.0 + nrm(ks[14], (D_MODEL,), 0.02),
    }


def reference(x, positions, norm1_g, w_in, w_gate_up, b_gate, gla_norm_g, w_pool, pool_scale,
              w_out, norm2_g, w_up, conv_w, conv_b, w_down, final_norm_g):
    cos, sin = rope_tables(positions)
    for l in range(DEPTH):
        x = hybrid_layer(x, cos, sin, norm1_g[l], w_in[l], w_gate_up[l], b_gate[l], gla_norm_g[l],
                         w_pool[l], pool_scale[l], w_out[l], norm2_g[l], w_up[l], conv_w[l],
                         conv_b[l], w_down[l])
    return rms_norm(x, final_norm_g)
```

```python
import functools

import jax
import jax.numpy as jnp
from jax import lax
from jax.experimental import pallas as pl
from jax.experimental.pallas import tpu as pltpu

D_MODEL = 1024
A_HEADS = 8
HEAD_DIM = 64
A_WIDTH = A_HEADS * HEAD_DIM
IDX_HEADS = 4
IDX_DIM = 64
TOPK_MAX = 256
ATTN_SCALE = HEAD_DIM ** -0.5
IDX_SCALE = (IDX_DIM ** -0.5) * (IDX_HEADS ** -0.5)
ROPE_THETA = 500000.0
ROT_DIM = HEAD_DIM // 4
B_HEADS = 4
B_DK = 32
B_DV = 64
B_KWIDTH = B_HEADS * B_DK
B_VWIDTH = B_HEADS * B_DV
GATE_RANK = 16
GATE_TAU = 16.0
POOL_WINDOWS = (2, 4, 8, 16)
C_GROUPS = 4
C_GROUP_DIM = 64
C_WIDTH = C_GROUPS * C_GROUP_DIM
MIX_WIDTH = A_WIDTH + B_VWIDTH + C_WIDTH
D_FF = 2816
CONV_WIDTH = 3
EPS = 1e-6

LANES = 128
SUBLANES = 8
VMEM_LIMIT = 56 * 1024 * 1024

P16_QA, P16_KA, P16_VA, P16_QI, P16_KI = 0, 512, 1024, 1536, 1792
P16_WIDTH = 1920
P32_VB, P32_RB, P32_UC, P32_WI, P32_QB, P32_KB, P32_GL = 0, 256, 512, 768, 896, 1024, 1152
P32_WIDTH = 1280

PROJ_TM = 512
DSA_TQ = 128
DSA_KC = 512
DSA_ROUND_STEPS = 4
DSA_MAX_ROUNDS = 96
GLA_CT = 512
GLA_C = 64
MIX_TM = 512
POOL_HALO = 16
FFN_TM = 512
FFN_TF = 1408

NEG_BIG = -1e30
HIGHEST = lax.Precision.HIGHEST
_NT = (((1,), (1,)), ((), ()))
_TN = (((0,), (0,)), ((), ()))


def _rms(x, g):
    ms = jnp.mean(x * x, axis=-1, keepdims=True)
    return x * lax.rsqrt(ms + EPS) * g


def _proj_kernel(x_ref, pos_ref, g_ref, w_ref, tab_ref, o16_ref, o32_ref):
    h = _rms(x_ref[...], g_ref[...]).astype(jnp.bfloat16)
    ang = pos_ref[...].astype(jnp.float32) * tab_ref[0:1, :]
    cos = jnp.cos(ang)
    sin = jnp.sin(ang)
    sin_lo = sin * tab_ref[1:2, :]
    sin_hi = sin * tab_ref[2:3, :]

    def seg(a, b):
        return jnp.dot(h, w_ref[:, a:b], preferred_element_type=jnp.float32)

    def rope_store(p, col, scale=None):
        for g in range(p.shape[1] // LANES):
            xg = p[:, g * LANES:(g + 1) * LANES]
            half = ROT_DIM // 2
            r = xg * cos + pltpu.roll(xg, LANES - half, 1) * sin_lo + pltpu.roll(xg, half, 1) * sin_hi
            if scale is not None:
                r = r * scale
            o16_ref[:, col + g * LANES: col + (g + 1) * LANES] = r.astype(jnp.bfloat16)

    rope_store(seg(P16_QA, P16_KA), P16_QA, ATTN_SCALE)
    rope_store(seg(P16_KA, P16_VA), P16_KA)
    o16_ref[:, P16_VA:P16_QI] = seg(P16_VA, P16_QI).astype(jnp.bfloat16)
    rope_store(seg(P16_QI, P16_KI), P16_QI)
    rope_store(seg(P16_KI, P16_WIDTH), P16_KI)
    base = P16_WIDTH
    o32_ref[:, P32_VB:P32_WI] = seg(base + P32_VB, base + P32_WI)
    o32_ref[:, P32_WI:P32_QB] = seg(base + P32_WI, base + P32_QB) * IDX_SCALE
    o32_ref[:, P32_QB:P32_WIDTH] = seg(base + P32_QB, base + P32_WIDTH)


def _proj(x2, pos2, g, w_pad, tab):
    T = x2.shape[0]
    tm = PROJ_TM
    return pl.pallas_call(
        _proj_kernel,
        grid=(T // tm,),
        in_specs=[
            pl.BlockSpec((tm, D_MODEL), lambda i: (i, 0)),
            pl.BlockSpec((tm, 1), lambda i: (i, 0)),
            pl.BlockSpec((1, D_MODEL), lambda i: (0, 0)),
            pl.BlockSpec((D_MODEL, P16_WIDTH + P32_WIDTH), lambda i: (0, 0)),
            pl.BlockSpec((SUBLANES, LANES), lambda i: (0, 0)),
        ],
        out_specs=[
            pl.BlockSpec((tm, P16_WIDTH), lambda i: (i, 0)),
            pl.BlockSpec((tm, P32_WIDTH), lambda i: (i, 0)),
        ],
        out_shape=[
            jax.ShapeDtypeStruct((T, P16_WIDTH), jnp.bfloat16),
            jax.ShapeDtypeStruct((T, P32_WIDTH), jnp.float32),
        ],
        compiler_params=pltpu.CompilerParams(
            dimension_semantics=("arbitrary",), vmem_limit_bytes=VMEM_LIMIT),
        name="proj",
    )(x2, pos2, g, w_pad, tab)


def _dsa_kernel(q_ref, k_ref, v_ref, qi_ref, ki_ref, wi_ref, o_ref,
                s_ref, qz_ref, m_ref, l_ref, acc_ref, *, n_keep):
    TQ, KC = DSA_TQ, DSA_KC
    i = pl.program_id(1)
    q0 = i * TQ
    nkc = (q0 + TQ + KC - 1) // KC
    kf = jnp.float32(n_keep)

    lane = lax.broadcasted_iota(jnp.int32, (1, LANES), 1)
    first_half = lane < HEAD_DIM
    tpos = (q0 + lax.broadcasted_iota(jnp.int32, (TQ, 1), 0)).astype(jnp.float32)
    kidx0 = lax.broadcasted_iota(jnp.int32, (1, KC), 1).astype(jnp.float32)

    zero16 = jnp.zeros((), jnp.bfloat16)
    qi = qi_ref[...]
    qiz = jnp.concatenate(
        [jnp.where(first_half if h % 2 == 0 else ~first_half, qi[:, (h // 2) * LANES:(h // 2 + 1) * LANES], zero16)
         for h in range(IDX_HEADS)], axis=0)
    wi = wi_ref[...]
    w_cols = [wi[:, h:h + 1] for h in range(IDX_HEADS)]

    def score_body(c, carry):
        rmin, rmax = carry
        r0 = pl.multiple_of(c * KC, KC)
        kic = ki_ref[pl.ds(r0, KC), :]
        logits = lax.dot_general(qiz, kic, _NT, preferred_element_type=jnp.float32)
        sc = w_cols[0] * jnp.maximum(logits[0:TQ], 0.0)
        for h in range(1, IDX_HEADS):
            sc = sc + w_cols[h] * jnp.maximum(logits[h * TQ:(h + 1) * TQ], 0.0)
        causal = (kidx0 + (c * KC).astype(jnp.float32)) <= tpos
        s_ref[c] = jnp.where(causal, sc, -jnp.inf)
        rmin = jnp.minimum(rmin, jnp.min(jnp.where(causal, sc, jnp.inf), axis=1, keepdims=True))
        rmax = jnp.maximum(rmax, jnp.max(jnp.where(causal, sc, -jnp.inf), axis=1, keepdims=True))
        return rmin, rmax

    rmin, rmax = lax.fori_loop(
        0, nkc, score_body,
        (jnp.full((TQ, 1), jnp.inf, jnp.float32), jnp.full((TQ, 1), -jnp.inf, jnp.float32)))

    def fold(m):
        p = m[:, 0:LANES]
        for g in range(1, KC // LANES):
            p = p + m[:, g * LANES:(g + 1) * LANES]
        return p

    def count_gt(x):
        def body(c, acc):
            return acc + fold(jnp.where(s_ref[c] > x, 1.0, 0.0))
        acc = lax.fori_loop(0, nkc, body, jnp.zeros((TQ, LANES), jnp.float32))
        return jnp.sum(acc, axis=1, keepdims=True)

    def min_above(x):
        def body(c, acc):
            blk = s_ref[c]
            m = jnp.where(blk > x, blk, jnp.inf)
            p = m[:, 0:LANES]
            for g in range(1, KC // LANES):
                p = jnp.minimum(p, m[:, g * LANES:(g + 1) * LANES])
            return jnp.minimum(acc, p)
        acc = lax.fori_loop(0, nkc, body, jnp.full((TQ, LANES), jnp.inf, jnp.float32))
        return jnp.min(acc, axis=1, keepdims=True)

    n_causal = tpos + 1.0
    all_kept = n_causal <= kf
    done0 = jnp.where(all_kept, 1.0, 0.0)
    lo0 = -(2.0 * jnp.abs(rmin) + 1.0)
    hi0 = rmax
    thr0 = jnp.full((TQ, 1), -jnp.inf, jnp.float32)

    def round_cond(st):
        return jnp.logical_and(st[0] < DSA_MAX_ROUNDS, st[1] > 0.0)

    def round_body(st):
        it, _, lo, hi, clo, chi, thr, cgt, done = st
        for r in range(DSA_ROUND_STEPS):
            if r % 2 == 0:
                frac = (clo - kf + 0.5) / jnp.maximum(clo - chi, 1.0)
                frac = jnp.minimum(jnp.maximum(frac, 0.0), 1.0)
                mid = lo + frac * (hi - lo)
            else:
                mid = 0.5 * lo + 0.5 * hi
            mid = jnp.minimum(jnp.maximum(mid, lo), hi)
            c = count_gt(mid)
            ge = c >= kf
            lo = jnp.where(ge, mid, lo)
            clo = jnp.where(ge, c, clo)
            hi = jnp.where(ge, hi, mid)
            chi = jnp.where(ge, chi, c)
        cand = min_above(lo)
        c2 = count_gt(cand)
        ok = c2 < kf
        newly = jnp.logical_and(ok, done < 0.5)
        thr = jnp.where(newly, cand, thr)
        cgt = jnp.where(newly, c2, cgt)
        done = jnp.where(ok, 1.0, done)
        lo = jnp.where(ok, lo, cand)
        clo = jnp.where(ok, clo, c2)
        return (it + 1, jnp.sum(1.0 - done), lo, hi, clo, chi, thr, cgt, done)

    st = lax.while_loop(
        round_cond, round_body,
        (jnp.int32(0), jnp.sum(1.0 - done0), lo0, hi0, n_causal, jnp.zeros((TQ, 1), jnp.float32),
         thr0, jnp.zeros((TQ, 1), jnp.float32), done0))
    thr, cgt = st[6], st[7]

    def count_eq(x):
        def body(c, acc):
            return acc + fold(jnp.where(s_ref[c] == x, 1.0, 0.0))
        acc = lax.fori_loop(0, nkc, body, jnp.zeros((TQ, LANES), jnp.float32))
        return jnp.sum(acc, axis=1, keepdims=True)

    need = kf - cgt
    ties = count_eq(thr)
    excess = jnp.where(jnp.logical_and(ties > need, jnp.logical_not(all_kept)), 1.0, 0.0)
    jall = jnp.where(all_kept, -1.0, jnp.float32(1e9))

    def tie_search(_):
        def count_tie_le(j):
            def body(c, acc):
                kidx = kidx0 + (c * KC).astype(jnp.float32)
                m = jnp.where(s_ref[c] == thr, jnp.where(kidx <= j, 1.0, 0.0), 0.0)
                return acc + fold(m)
            acc = lax.fori_loop(0, nkc, body, jnp.zeros((TQ, LANES), jnp.float32))
            return jnp.sum(acc, axis=1, keepdims=True)

        def body(_, lh):
            jl, jh = lh
            jm = jnp.floor(0.5 * (jl + jh))
            ge = count_tie_le(jm) >= need
            return jnp.where(ge, jl, jm), jnp.where(ge, jm, jh)

        n_iter = max(1, int(k_ref.shape[0] - 1).bit_length()) + 1
        jl, jh = lax.fori_loop(
            0, n_iter, body,
            (jnp.full((TQ, 1), -1.0, jnp.float32), jnp.zeros((TQ, 1), jnp.float32) + (q0 + TQ - 1).astype(jnp.float32)))
        return jnp.where(excess > 0.5, jh, jall)

    jstar = lax.cond(jnp.sum(excess) > 0.0, tie_search, lambda _: jall, 0)

    zq = jnp.zeros((), q_ref.dtype)
    for p in range(A_HEADS // 2):
        qp = q_ref[:, p * LANES:(p + 1) * LANES]
        qz_ref[p, 0:TQ, :] = jnp.where(first_half, qp, zq)
        qz_ref[p, TQ:2 * TQ, :] = jnp.where(first_half, zq, qp)
    m_ref[...] = jnp.full(m_ref.shape, NEG_BIG, jnp.float32)
    l_ref[...] = jnp.zeros(l_ref.shape, jnp.float32)
    acc_ref[...] = jnp.zeros(acc_ref.shape, jnp.float32)

    def attn_body(c, carry):
        r0 = pl.multiple_of(c * KC, KC)
        sc = s_ref[c]
        kidx = kidx0 + (c * KC).astype(jnp.float32)
        bias = jnp.where(sc > thr, 0.0,
                         jnp.where(sc == thr, jnp.where(kidx <= jstar, 0.0, NEG_BIG), NEG_BIG))
        bias2 = jnp.concatenate([bias, bias], axis=0)
        for p in range(A_HEADS // 2):
            kc = k_ref[pl.ds(r0, KC), p * LANES:(p + 1) * LANES]
            vc = v_ref[pl.ds(r0, KC), p * LANES:(p + 1) * LANES]
            s = lax.dot_general(qz_ref[p], kc, _NT, preferred_element_type=jnp.float32) + bias2
            m_old = m_ref[p]
            m_new = jnp.maximum(m_old, jnp.max(s, axis=1, keepdims=True))
            alpha = jnp.exp(m_old - m_new)
            pe = jnp.exp(s - m_new)
            l_ref[p] = alpha * l_ref[p] + jnp.sum(pe, axis=1, keepdims=True)
            acc_ref[p] = alpha * acc_ref[p] + jnp.dot(pe.astype(jnp.bfloat16), vc,
                                                      preferred_element_type=jnp.float32)
            m_ref[p] = m_new
        return carry

    lax.fori_loop(0, nkc, attn_body, 0)

    for p in range(A_HEADS // 2):
        o = acc_ref[p] / l_ref[p]
        o_ref[:, p * LANES:(p + 1) * LANES] = jnp.where(first_half, o[0:TQ], o[TQ:2 * TQ]).astype(o_ref.dtype)


def _dsa(p16, p32, nbat, S):
    T = nbat * S
    TQ, KC = DSA_TQ, DSA_KC
    nq = S // TQ
    n_keep = min(TOPK_MAX, S // 4)
    once = pl.Buffered(1)
    return pl.pallas_call(
        functools.partial(_dsa_kernel, n_keep=n_keep),
        grid=(nbat, nq),
        in_specs=[
            pl.BlockSpec((TQ, A_WIDTH), lambda b, i: (b * nq + i, P16_QA // A_WIDTH)),
            pl.BlockSpec((S, A_WIDTH), lambda b, i: (b, P16_KA // A_WIDTH), pipeline_mode=once),
            pl.BlockSpec((S, A_WIDTH), lambda b, i: (b, P16_VA // A_WIDTH), pipeline_mode=once),
            pl.BlockSpec((TQ, 2 * LANES), lambda b, i: (b * nq + i, P16_QI // (2 * LANES))),
            pl.BlockSpec((S, LANES), lambda b, i: (b, P16_KI // LANES), pipeline_mode=once),
            pl.BlockSpec((TQ, LANES), lambda b, i: (b * nq + i, P32_WI // LANES)),
        ],
        out_specs=pl.BlockSpec((TQ, A_WIDTH), lambda b, i: (b * nq + i, 0)),
        out_shape=jax.ShapeDtypeStruct((T, A_WIDTH), jnp.bfloat16),
        scratch_shapes=[
            pltpu.VMEM((pl.cdiv(S, KC), TQ, KC), jnp.float32),
            pltpu.VMEM((A_HEADS // 2, 2 * TQ, LANES), jnp.bfloat16),
            pltpu.VMEM((A_HEADS // 2, 2 * TQ, 1), jnp.float32),
            pltpu.VMEM((A_HEADS // 2, 2 * TQ, 1), jnp.float32),
            pltpu.VMEM((A_HEADS // 2, 2 * TQ, LANES), jnp.float32),
        ],
        compiler_params=pltpu.CompilerParams(
            dimension_semantics=("arbitrary", "arbitrary"), vmem_limit_bytes=VMEM_LIMIT),
        name="dsa",
    )(p16, p16, p16, p16, p16, p32)


def _gla_kernel(q_ref, k_ref, v_ref, r_ref, gl_ref, wg_ref, bg_ref, gn_ref, o_ref, st_ref):
    C = GLA_C

    @pl.when(pl.program_id(1) == 0)
    def _():
        st_ref[...] = jnp.zeros(st_ref.shape, jnp.float32)

    row = lax.broadcasted_iota(jnp.int32, (C, C), 0)
    col = lax.broadcasted_iota(jnp.int32, (C, C), 1)
    tril = row >= col
    tril_f = jnp.where(tril, 1.0, 0.0)
    tril4 = jnp.concatenate([tril] * B_HEADS, axis=0)
    khead = lax.broadcasted_iota(jnp.int32, (1, B_KWIDTH), 1) // B_DK
    vhead = lax.broadcasted_iota(jnp.int32, (1, B_VWIDTH), 1) // B_DV
    st_rows = lax.broadcasted_iota(jnp.int32, (B_VWIDTH, B_KWIDTH), 0) // B_DV
    st_cols = lax.broadcasted_iota(jnp.int32, (B_VWIDTH, B_KWIDTH), 1) // B_DK
    same_head = st_rows == st_cols
    seg_r = lax.broadcasted_iota(jnp.int32, (B_VWIDTH, B_VWIDTH), 0) // B_DV
    seg_c = lax.broadcasted_iota(jnp.int32, (B_VWIDTH, B_VWIDTH), 1) // B_DV
    seg_mean = jnp.where(seg_r == seg_c, 1.0 / B_DV, 0.0)
    wg = wg_ref[...]
    bg = bg_ref[...]
    gn = gn_ref[...]

    def chunk(ci, carry):
        r0 = pl.multiple_of(ci * C, C)
        q = q_ref[pl.ds(r0, C), :]
        k = k_ref[pl.ds(r0, C), :]
        v = v_ref[pl.ds(r0, C), :].astype(jnp.bfloat16)
        z = jnp.dot(gl_ref[pl.ds(r0, C), :], wg, precision=HIGHEST, preferred_element_type=jnp.float32) + bg
        log_a = (jnp.minimum(z, 0.0) - jnp.log1p(jnp.exp(-jnp.abs(z)))) / GATE_TAU
        b = jnp.dot(tril_f, log_a, precision=HIGHEST, preferred_element_type=jnp.float32)
        b_last = b[C - 1:C, :]
        qt = (q * (B_DK ** -0.5)) * jnp.exp(b)
        kt = (k * jnp.exp(-b)).astype(jnp.bfloat16)
        kh = (k * jnp.exp(b_last - b)).astype(jnp.bfloat16)
        qz = jnp.concatenate([jnp.where(khead == h, qt, 0.0) for h in range(B_HEADS)], axis=0)
        att = lax.dot_general(qz.astype(jnp.bfloat16), kt, _NT, preferred_element_type=jnp.float32)
        att = jnp.where(tril4, att, 0.0).astype(jnp.bfloat16)
        res = jnp.dot(att, v, preferred_element_type=jnp.float32)
        st = st_ref[...]
        o = lax.dot_general(qt.astype(jnp.bfloat16), st.astype(jnp.bfloat16), _NT,
                            preferred_element_type=jnp.float32)
        for h in range(B_HEADS):
            o = o + jnp.where(vhead == h, res[h * C:(h + 1) * C], 0.0)
        upd = lax.dot_general(v, kh, _TN, preferred_element_type=jnp.float32)
        st_ref[...] = st * jnp.exp(b_last) + jnp.where(same_head, upd, 0.0)
        ms = jnp.dot(o * o, seg_mean, precision=HIGHEST, preferred_element_type=jnp.float32)
        y = o * lax.rsqrt(ms + EPS) * gn
        r = r_ref[pl.ds(r0, C), :]
        o_ref[pl.ds(r0, C), :] = (y * (r * jax.nn.sigmoid(r))).astype(o_ref.dtype)
        return carry

    lax.fori_loop(0, GLA_CT // C, chunk, 0)


def _gla(p32, wg_pad, bg, gn, nbat, S):
    T = nbat * S
    CT = GLA_CT
    nt = S // CT
    row = lambda b, i: b * nt + i
    return pl.pallas_call(
        _gla_kernel,
        grid=(nbat, nt),
        in_specs=[
            pl.BlockSpec((CT, B_KWIDTH), lambda b, i: (row(b, i), P32_QB // B_KWIDTH)),
            pl.BlockSpec((CT, B_KWIDTH), lambda b, i: (row(b, i), P32_KB // B_KWIDTH)),
            pl.BlockSpec((CT, B_VWIDTH), lambda b, i: (row(b, i), P32_VB // B_VWIDTH)),
            pl.BlockSpec((CT, B_VWIDTH), lambda b, i: (row(b, i), P32_RB // B_VWIDTH)),
            pl.BlockSpec((CT, LANES), lambda b, i: (row(b, i), P32_GL // LANES)),
            pl.BlockSpec((LANES, B_KWIDTH), lambda b, i: (0, 0)),
            pl.BlockSpec((1, B_KWIDTH), lambda b, i: (0, 0)),
            pl.BlockSpec((1, B_VWIDTH), lambda b, i: (0, 0)),
        ],
        out_specs=pl.BlockSpec((CT, B_VWIDTH), lambda b, i: (row(b, i), 0)),
        out_shape=jax.ShapeDtypeStruct((T, B_VWIDTH), jnp.bfloat16),
        scratch_shapes=[pltpu.VMEM((B_VWIDTH, B_KWIDTH), jnp.float32)],
        compiler_params=pltpu.CompilerParams(
            dimension_semantics=("arbitrary", "arbitrary"), vmem_limit_bytes=VMEM_LIMIT),
        name="gla",
    )(p32, p32, p32, p32, p32, wg_pad, bg, gn)


def _mix_kernel(x_ref, oa_ref, ob_ref, uc_ref, uh_ref, wo_ref, wp_ref, ps_ref, o_ref, ext_ref, *, seq):
    TM = MIX_TM
    t0 = (pl.program_id(0) * TM) % seq
    uc = uc_ref[...]
    @pl.when(t0 == 0)
    def _():
        ext_ref[0:POOL_HALO, :] = jnp.zeros((POOL_HALO, C_WIDTH), jnp.float32)

    @pl.when(t0 != 0)
    def _():
        ext_ref[0:POOL_HALO, :] = uh_ref[...]

    ext_ref[POOL_HALO:POOL_HALO + TM, :] = uc
    lane = lax.broadcasted_iota(jnp.int32, (1, C_WIDTH), 1)
    group = lane // C_GROUP_DIM
    wsum = jnp.zeros_like(uc)
    cum = uc
    for j in range(1, max(POOL_WINDOWS)):
        cum = cum + ext_ref[POOL_HALO - j:POOL_HALO - j + TM, :]
        if (j + 1) in POOL_WINDOWS:
            wsum = jnp.where(group == POOL_WINDOWS.index(j + 1), cum, wsum)
    win = jnp.zeros((1, C_WIDTH), jnp.float32)
    for g, w in enumerate(POOL_WINDOWS):
        win = jnp.where(group == g, float(w), win)
    tpos = (t0 + lax.broadcasted_iota(jnp.int32, (TM, 1), 0)).astype(jnp.float32)
    cnt = jnp.minimum(tpos + 1.0, win)
    pooled = wsum / cnt - uc
    y = jnp.dot(pooled.astype(jnp.bfloat16), wp_ref[...], preferred_element_type=jnp.float32) * ps_ref[...]
    mix = jnp.dot(oa_ref[...], wo_ref[0:A_WIDTH, :], preferred_element_type=jnp.float32)
    mix = mix + jnp.dot(ob_ref[...], wo_ref[A_WIDTH:A_WIDTH + B_VWIDTH, :], preferred_element_type=jnp.float32)
    mix = mix + jnp.dot(y.astype(jnp.bfloat16), wo_ref[A_WIDTH + B_VWIDTH:MIX_WIDTH, :],
                        preferred_element_type=jnp.float32)
    o_ref[...] = x_ref[...] + mix


def _mix(x2, o_a, o_b, p32, w_out, wp_bd, ps, S):
    T = x2.shape[0]
    TM = MIX_TM
    per = TM // POOL_HALO
    return pl.pallas_call(
        functools.partial(_mix_kernel, seq=S),
        grid=(T // TM,),
        in_specs=[
            pl.BlockSpec((TM, D_MODEL), lambda i: (i, 0)),
            pl.BlockSpec((TM, A_WIDTH), lambda i: (i, 0)),
            pl.BlockSpec((TM, B_VWIDTH), lambda i: (i, 0)),
            pl.BlockSpec((TM, C_WIDTH), lambda i: (i, P32_UC // C_WIDTH)),
            pl.BlockSpec((POOL_HALO, C_WIDTH), lambda i: (jnp.maximum(i * per - 1, 0), P32_UC // C_WIDTH)),
            pl.BlockSpec((MIX_WIDTH, D_MODEL), lambda i: (0, 0)),
            pl.BlockSpec((C_WIDTH, C_WIDTH), lambda i: (0, 0)),
            pl.BlockSpec((1, C_WIDTH), lambda i: (0, 0)),
        ],
        out_specs=pl.BlockSpec((TM, D_MODEL), lambda i: (i, 0)),
        out_shape=jax.ShapeDtypeStruct((T, D_MODEL), jnp.float32),
        scratch_shapes=[pltpu.VMEM((POOL_HALO + TM, C_WIDTH), jnp.float32)],
        compiler_params=pltpu.CompilerParams(
            dimension_semantics=("arbitrary",), vmem_limit_bytes=VMEM_LIMIT),
        name="mix",
    )(x2, o_a, o_b, p32, p32, w_out, wp_bd, ps)


def _ffn_kernel(x_ref, g_ref, wa_ref, wb_ref, cw_ref, cb_ref, wd_ref, fg_ref, o_ref,
                h_ref, acc_ref, ext_ref, carry_ref, *, seq, final):
    TM = FFN_TM
    i = pl.program_id(0)
    j = pl.program_id(1)
    t0 = (i * TM) % seq

    @pl.when(j == 0)
    def _():
        h_ref[...] = _rms(x_ref[...], g_ref[...]).astype(jnp.bfloat16)
        acc_ref[...] = jnp.zeros(acc_ref.shape, jnp.float32)

    h = h_ref[...]
    a = jnp.dot(h, wa_ref[...], preferred_element_type=jnp.float32)
    bgate = jnp.dot(h, wb_ref[...], preferred_element_type=jnp.float32)
    @pl.when(t0 == 0)
    def _():
        ext_ref[0:SUBLANES, :] = jnp.zeros((SUBLANES, ext_ref.shape[1]), jnp.float32)

    @pl.when(t0 != 0)
    def _():
        ext_ref[0:SUBLANES, :] = carry_ref[j]

    ext_ref[SUBLANES:SUBLANES + TM, :] = a
    carry_ref[j] = a[TM - SUBLANES:TM, :]
    conv = cb_ref[...] + a * cw_ref[CONV_WIDTH - 1:CONV_WIDTH, :]
    for d in range(1, CONV_WIDTH):
        conv = conv + ext_ref[SUBLANES - d:SUBLANES - d + TM, :] * cw_ref[CONV_WIDTH - 1 - d:CONV_WIDTH - d, :]
    gate = (conv * jax.nn.sigmoid(conv) * bgate).astype(jnp.bfloat16)
    acc_ref[...] += jnp.dot(gate, wd_ref[...], preferred_element_type=jnp.float32)

    @pl.when(j == pl.num_programs(1) - 1)
    def _():
        y = x_ref[...] + acc_ref[...]
        if final:
            y = _rms(y, fg_ref[...])
        o_ref[...] = y


def _ffn(x2, g, w_up, conv_w, conv_b, w_down, fg, S, final):
    T = x2.shape[0]
    TM, TF = FFN_TM, FFN_TF
    nj = D_FF // TF
    return pl.pallas_call(
        functools.partial(_ffn_kernel, seq=S, final=final),
        grid=(T // TM, nj),
        in_specs=[
            pl.BlockSpec((TM, D_MODEL), lambda i, j: (i, 0)),
            pl.BlockSpec((1, D_MODEL), lambda i, j: (0, 0)),
            pl.BlockSpec((D_MODEL, TF), lambda i, j: (0, j)),
            pl.BlockSpec((D_MODEL, TF), lambda i, j: (0, nj + j)),
            pl.BlockSpec((CONV_WIDTH, TF), lambda i, j: (0, j)),
            pl.BlockSpec((1, TF), lambda i, j: (0, j)),
            pl.BlockSpec((TF, D_MODEL), lambda i, j: (j, 0)),
            pl.BlockSpec((1, D_MODEL), lambda i, j: (0, 0)),
        ],
        out_specs=pl.BlockSpec((TM, D_MODEL), lambda i, j: (i, 0)),
        out_shape=jax.ShapeDtypeStruct((T, D_MODEL), jnp.float32),
        scratch_shapes=[
            pltpu.VMEM((TM, D_MODEL), jnp.bfloat16),
            pltpu.VMEM((TM, D_MODEL), jnp.float32),
            pltpu.VMEM((SUBLANES + TM, TF), jnp.float32),
            pltpu.VMEM((nj, SUBLANES, TF), jnp.float32),
        ],
        compiler_params=pltpu.CompilerParams(
            dimension_semantics=("arbitrary", "arbitrary"), vmem_limit_bytes=VMEM_LIMIT),
        name="ffn_final" if final else "ffn",
    )(x2, g, w_up, w_up, conv_w, conv_b, w_down, fg)


def _pad_cols(w, width):
    return jnp.pad(w, ((0, 0), (0, width - w.shape[1])))


def _pack_w_in(w):
    sizes = (A_WIDTH, A_WIDTH, A_WIDTH, IDX_HEADS * IDX_DIM, IDX_DIM, IDX_HEADS,
             B_KWIDTH, B_KWIDTH, B_VWIDTH, B_VWIDTH, GATE_RANK, C_WIDTH)
    parts, off = [], 0
    for n in sizes:
        parts.append(w[:, off:off + n])
        off += n
    qa, ka, va, qi, ki, wi, qb, kb, vb, rb, gl, uc = parts
    cols = [qa, ka, va, qi, ki, ki, vb, rb, uc, _pad_cols(wi, LANES), qb, kb, _pad_cols(gl, LANES)]
    return jnp.concatenate(cols, axis=1).astype(jnp.bfloat16)


def _rope_table():
    lane = jnp.arange(LANES) % HEAD_DIM
    inv = ROPE_THETA ** (-jnp.arange(0, ROT_DIM, 2, dtype=jnp.float32) / ROT_DIM)
    half = ROT_DIM // 2
    inv_lane = jnp.where(lane < ROT_DIM, inv[lane % half], 0.0)
    lo = jnp.where(lane < half, -1.0, 0.0)
    hi = jnp.where((lane >= half) & (lane < ROT_DIM), 1.0, 0.0)
    tab = jnp.zeros((SUBLANES, LANES), jnp.float32)
    return tab.at[0].set(inv_lane).at[1].set(lo).at[2].set(hi)


def kernel(x, positions, norm1_g, w_in, w_gate_up, b_gate, gla_norm_g, w_pool, pool_scale, w_out, norm2_g, w_up, conv_w, conv_b, w_down, final_norm_g):
    nbat, S, _ = x.shape
    T = nbat * S
    depth = w_in.shape[0]
    assert S % DSA_KC == 0 and S % GLA_CT == 0 and S % MIX_TM == 0 and S % FFN_TM == 0 and T % PROJ_TM == 0
    x2 = x.reshape(T, D_MODEL)
    pos2 = positions.reshape(T, 1)
    tab = _rope_table()
    fg = final_norm_g.reshape(1, D_MODEL)
    bf = jnp.bfloat16
    for l in range(depth):
        p16, p32 = _proj(x2, pos2, norm1_g[l].reshape(1, D_MODEL), _pack_w_in(w_in[l]), tab)
        o_a = _dsa(p16, p32, nbat, S)
        wg_pad = jnp.pad(w_gate_up[l], ((0, LANES - GATE_RANK), (0, 0)))
        o_b = _gla(p32, wg_pad, b_gate[l].reshape(1, B_KWIDTH), gla_norm_g[l].reshape(1, B_VWIDTH), nbat, S)
        wp_bd = jax.scipy.linalg.block_diag(*[w_pool[l, g] for g in range(C_GROUPS)]).astype(bf)
        x2 = _mix(x2, o_a, o_b, p32, w_out[l].astype(bf), wp_bd, pool_scale[l].reshape(1, C_WIDTH), S)
        x2 = _ffn(x2, norm2_g[l].reshape(1, D_MODEL), w_up[l].astype(bf), conv_w[l],
                  conv_b[l].reshape(1, D_FF), w_down[l].astype(bf), fg, S, final=(l == depth - 1))
    return x2.reshape(nbat, S, D_MODEL)
```

```python
import functools

import jax
import jax.numpy as jnp
from jax import lax
from jax.experimental import pallas as pl
from jax.experimental.pallas import tpu as pltpu

D_MODEL = 1024
A_HEADS = 8
HEAD_DIM = 64
A_WIDTH = A_HEADS * HEAD_DIM
IDX_HEADS = 4
IDX_DIM = 64
TOPK_MAX = 256
ATTN_SCALE = HEAD_DIM ** -0.5
IDX_SCALE = (IDX_DIM ** -0.5) * (IDX_HEADS ** -0.5)
ROPE_THETA = 500000.0
ROT_DIM = HEAD_DIM // 4
B_HEADS = 4
B_DK = 32
B_DV = 64
B_KWIDTH = B_HEADS * B_DK
B_VWIDTH = B_HEADS * B_DV
GATE_RANK = 16
GATE_TAU = 16.0
POOL_WINDOWS = (2, 4, 8, 16)
C_GROUPS = 4
C_GROUP_DIM = 64
C_WIDTH = C_GROUPS * C_GROUP_DIM
MIX_WIDTH = A_WIDTH + B_VWIDTH + C_WIDTH
D_FF = 2816
CONV_WIDTH = 3
EPS = 1e-6

LANES = 128
SUBLANES = 8
VMEM_LIMIT = 56 * 1024 * 1024

P16_QA, P16_KA, P16_QI, P16_KI = 0, 512, 1024, 1280
P16_WIDTH = 1408
PW_32 = P16_WIDTH
P32_VB, P32_RB, P32_UC, P32_WI, P32_QB, P32_KB, P32_GL = 0, 256, 512, 768, 896, 1024, 1152
P32_WIDTH = 1280

PROJ_TM = 512
DSA_TQ = 128
DSA_KC = 512
DSA_ACC = 32
DSA_CLS = 128
DSA_FIRST_STEPS = 7
DSA_ROUND_STEPS = 2
DSA_RANK_MAX = 8
DSA_MAX_ROUNDS = 96
GLA_CT = 512
GLA_C = 64
MIX_TM = 512
POOL_HALO = 16
FFN_TM = 512
FFN_TF = 1408

NEG_BIG = -1e30
HIGHEST = lax.Precision.HIGHEST
_NT = (((1,), (1,)), ((), ()))
_TN = (((0,), (0,)), ((), ()))


def _rms(x, g):
    ms = jnp.mean(x * x, axis=-1, keepdims=True)
    return x * lax.rsqrt(ms + EPS) * g


def _proj_kernel(x_ref, pos_ref, g_ref, w_ref, wvt_ref, tab_ref, o16_ref, vt_ref, o32_ref):
    h = _rms(x_ref[...], g_ref[...]).astype(jnp.bfloat16)
    ang = pos_ref[...].astype(jnp.float32) * tab_ref[0:1, :]
    cos = jnp.cos(ang)
    sin = jnp.sin(ang)
    sin_lo = sin * tab_ref[1:2, :]
    sin_hi = sin * tab_ref[2:3, :]

    def seg(a, b):
        return jnp.dot(h, w_ref[:, a:b], preferred_element_type=jnp.float32)

    def rope_store(p, col, scale=None):
        for g in range(p.shape[1] // LANES):
            xg = p[:, g * LANES:(g + 1) * LANES]
            half = ROT_DIM // 2
            r = xg * cos + pltpu.roll(xg, LANES - half, 1) * sin_lo + pltpu.roll(xg, half, 1) * sin_hi
            if scale is not None:
                r = r * scale
            o16_ref[:, col + g * LANES: col + (g + 1) * LANES] = r.astype(jnp.bfloat16)

    rope_store(seg(P16_QA, P16_KA), P16_QA, ATTN_SCALE)
    rope_store(seg(P16_KA, P16_QI), P16_KA)
    rope_store(seg(P16_QI, P16_KI), P16_QI)
    rope_store(seg(P16_KI, P16_WIDTH), P16_KI)
    vt_ref[0] = lax.dot_general(wvt_ref[...], h, _NT,
                                preferred_element_type=jnp.float32).astype(jnp.bfloat16)
    base = PW_32
    o32_ref[:, P32_VB:P32_WI] = seg(base + P32_VB, base + P32_WI)
    o32_ref[:, P32_WI:P32_QB] = seg(base + P32_WI, base + P32_QB) * IDX_SCALE
    o32_ref[:, P32_QB:P32_WIDTH] = seg(base + P32_QB, base + P32_WIDTH)


def _proj(x2, pos2, g, w_pad, wvt, tab):
    T = x2.shape[0]
    tm = PROJ_TM
    return pl.pallas_call(
        _proj_kernel,
        grid=(T // tm,),
        in_specs=[
            pl.BlockSpec((tm, D_MODEL), lambda i: (i, 0)),
            pl.BlockSpec((tm, 1), lambda i: (i, 0)),
            pl.BlockSpec((1, D_MODEL), lambda i: (0, 0)),
            pl.BlockSpec((D_MODEL, PW_32 + P32_WIDTH), lambda i: (0, 0)),
            pl.BlockSpec((A_WIDTH, D_MODEL), lambda i: (0, 0)),
            pl.BlockSpec((SUBLANES, LANES), lambda i: (0, 0)),
        ],
        out_specs=[
            pl.BlockSpec((tm, P16_WIDTH), lambda i: (i, 0)),
            pl.BlockSpec((1, A_WIDTH, tm), lambda i: (i, 0, 0)),
            pl.BlockSpec((tm, P32_WIDTH), lambda i: (i, 0)),
        ],
        out_shape=[
            jax.ShapeDtypeStruct((T, P16_WIDTH), jnp.bfloat16),
            jax.ShapeDtypeStruct((T // tm, A_WIDTH, tm), jnp.bfloat16),
            jax.ShapeDtypeStruct((T, P32_WIDTH), jnp.float32),
        ],
        compiler_params=pltpu.CompilerParams(
            dimension_semantics=("arbitrary",), vmem_limit_bytes=VMEM_LIMIT),
        name="proj",
    )(x2, pos2, g, w_pad, wvt, tab)


def _dsa_kernel(q_ref, k_ref, vt_ref, qi_ref, ki_ref, wi_ref, o_ref,
                s_ref, qz_ref, m_ref, l_ref, acc_ref, sb_ref, cm_ref, *, n_keep):
    TQ, KC = DSA_TQ, DSA_KC
    i = pl.program_id(1)
    q0 = i * TQ
    nkc = (q0 + TQ + KC - 1) // KC
    kf = jnp.float32(n_keep)

    lane = lax.broadcasted_iota(jnp.int32, (1, LANES), 1)
    first_half = lane < HEAD_DIM
    tpos = (q0 + lax.broadcasted_iota(jnp.int32, (1, TQ), 1)).astype(jnp.float32)
    kidx0 = lax.broadcasted_iota(jnp.int32, (KC, 1), 0).astype(jnp.float32)

    zero16 = jnp.zeros((), jnp.bfloat16)
    qi = qi_ref[...]
    qiz = jnp.concatenate(
        [jnp.where(first_half if h % 2 == 0 else ~first_half, qi[:, (h // 2) * LANES:(h // 2 + 1) * LANES], zero16)
         for h in range(IDX_HEADS)], axis=0)
    w_rows = wi_ref[...].T

    def score_body(c, carry):
        rmin, rmax = carry
        r0 = pl.multiple_of(c * KC, KC)
        kic = ki_ref[pl.ds(r0, KC), :]
        logits = lax.dot_general(kic, qiz, _NT, preferred_element_type=jnp.float32)
        sc = w_rows[0:1, :] * jnp.maximum(logits[:, 0:TQ], 0.0)
        for h in range(1, IDX_HEADS):
            sc = sc + w_rows[h:h + 1, :] * jnp.maximum(logits[:, h * TQ:(h + 1) * TQ], 0.0)
        causal = (kidx0 + (c * KC).astype(jnp.float32)) <= tpos
        s_ref[c] = jnp.where(causal, sc, -jnp.inf)
        rmin = jnp.minimum(rmin, jnp.min(jnp.where(causal, sc, jnp.inf), axis=0, keepdims=True))
        rmax = jnp.maximum(rmax, jnp.max(jnp.where(causal, sc, -jnp.inf), axis=0, keepdims=True))
        return rmin, rmax

    rmin, rmax = lax.fori_loop(
        0, nkc, score_body,
        (jnp.full((1, TQ), jnp.inf, jnp.float32), jnp.full((1, TQ), -jnp.inf, jnp.float32)))

    def fold(m):
        return jnp.sum(m.reshape(KC // DSA_ACC, DSA_ACC, TQ), axis=0)

    def count_gt(x):
        def body(c, acc):
            return acc + fold(jnp.where(s_ref[c] > x, 1.0, 0.0))
        acc = lax.fori_loop(0, nkc, body, jnp.zeros((DSA_ACC, TQ), jnp.float32))
        return jnp.sum(acc, axis=0, keepdims=True)

    def count_eq(x):
        def body(c, acc):
            return acc + fold(jnp.where(s_ref[c] == x, 1.0, 0.0))
        acc = lax.fori_loop(0, nkc, body, jnp.zeros((DSA_ACC, TQ), jnp.float32))
        return jnp.sum(acc, axis=0, keepdims=True)

    def class_min_above(x):
        def body(c, acc):
            blk = s_ref[c]
            m = jnp.where(blk > x, blk, jnp.inf)
            return jnp.minimum(acc, jnp.min(m.reshape(KC // DSA_CLS, DSA_CLS, TQ), axis=0))
        return lax.fori_loop(0, nkc, body, jnp.full((DSA_CLS, TQ), jnp.inf, jnp.float32))

    n_causal = tpos + 1.0
    all_kept = n_causal <= kf
    zeros_row = jnp.zeros((1, TQ), jnp.float32)
    log_target = jnp.log(kf + 0.5)

    def yval(cnt):
        return jnp.log(jnp.maximum(cnt, 0.5)) - log_target

    c0 = count_gt(zeros_row)
    e0 = count_eq(zeros_row)
    at_zero = jnp.logical_and(jnp.logical_and(c0 < kf, c0 + e0 >= kf), jnp.logical_not(all_kept))
    done0 = jnp.where(jnp.logical_or(all_kept, at_zero), 1.0, 0.0)
    thr0 = jnp.where(at_zero, 0.0, -jnp.inf)
    cgt0 = jnp.where(at_zero, c0, 0.0)
    ties0 = jnp.where(at_zero, e0, 0.0)
    above = c0 >= kf
    below = jnp.logical_and(jnp.logical_not(above), rmax > 0.0)
    lo0 = jnp.where(above, 0.0, -(2.0 * jnp.abs(rmin) + 1.0))
    clo0 = jnp.where(above, c0, n_causal)
    hi0 = jnp.where(below, 0.0, rmax)
    chi0 = jnp.where(below, c0 + e0, 0.0)

    def narrow(nst):
        lo, hi, clo, chi, ylo, yhi, side = nst
        frac = jnp.minimum(jnp.maximum(ylo / jnp.maximum(ylo - yhi, 1e-9), 0.0), 1.0)
        mid = jnp.minimum(jnp.maximum(lo + frac * (hi - lo), lo), hi)
        c = count_gt(mid)
        ge = c >= kf
        y = yval(c)
        yhi = jnp.where(jnp.logical_and(ge, side > 0.5), 0.5 * yhi, yhi)
        ylo = jnp.where(jnp.logical_and(jnp.logical_not(ge), side < -0.5), 0.5 * ylo, ylo)
        return (jnp.where(ge, mid, lo), jnp.where(ge, hi, mid), jnp.where(ge, c, clo), jnp.where(ge, chi, c),
                jnp.where(ge, y, ylo), jnp.where(ge, yhi, y), jnp.where(ge, 1.0, -1.0))

    nst = (lo0, hi0, clo0, chi0, yval(clo0), yval(chi0), zeros_row)
    nst = lax.fori_loop(0, DSA_FIRST_STEPS - DSA_ROUND_STEPS, lambda _, s: narrow(s), nst)

    def round_cond(st):
        return jnp.logical_and(st[0] < DSA_MAX_ROUNDS, st[1] > 0.0)

    def round_body(st):
        it, _, nst, thr, cgt, ties, done = st
        for _ in range(DSA_ROUND_STEPS):
            nst = narrow(nst)
        lo, hi, clo, chi, ylo, yhi, side = nst
        pm = class_min_above(lo)
        rank = clo - kf + 1.0
        cand = jnp.min(pm, axis=0, keepdims=True)
        for j in range(2, DSA_RANK_MAX + 1):
            pm = jnp.where(pm <= cand, jnp.inf, pm)
            nxt = jnp.min(pm, axis=0, keepdims=True)
            cand = jnp.where(jnp.logical_and(rank >= j, nxt < jnp.inf), nxt, cand)
        c2 = count_gt(cand)
        e2 = count_eq(cand)
        ok = jnp.logical_and(c2 < kf, c2 + e2 >= kf)
        newly = jnp.logical_and(ok, done < 0.5)
        thr = jnp.where(newly, cand, thr)
        cgt = jnp.where(newly, c2, cgt)
        ties = jnp.where(newly, e2, ties)
        done = jnp.where(ok, 1.0, done)
        up = c2 >= kf
        dn = c2 + e2 < kf
        nst = (jnp.where(up, cand, lo), jnp.where(dn, cand, hi), jnp.where(up, c2, clo),
               jnp.where(dn, c2 + e2, chi), jnp.where(up, yval(c2), ylo), jnp.where(dn, yval(c2 + e2), yhi), side)
        return (it + 1, jnp.sum(1.0 - done), nst, thr, cgt, ties, done)

    st = lax.while_loop(round_cond, round_body,
                        (jnp.int32(0), jnp.sum(1.0 - done0), nst, thr0, cgt0, ties0, done0))
    thr, cgt, ties = st[3], st[4], st[5]

    need = kf - cgt
    excess = jnp.where(jnp.logical_and(ties > need, jnp.logical_not(all_kept)), 1.0, 0.0)
    jall = jnp.where(all_kept, -1.0, jnp.float32(1e9))

    def tie_search(_):
        def count_tie_le(j):
            def body(c, acc):
                kidx = kidx0 + (c * KC).astype(jnp.float32)
                m = jnp.where(s_ref[c] == thr, jnp.where(kidx <= j, 1.0, 0.0), 0.0)
                return acc + fold(m)
            acc = lax.fori_loop(0, nkc, body, jnp.zeros((DSA_ACC, TQ), jnp.float32))
            return jnp.sum(acc, axis=0, keepdims=True)

        def body(_, lh):
            jl, jh = lh
            jm = jnp.floor(0.5 * (jl + jh))
            ge = count_tie_le(jm) >= need
            return jnp.where(ge, jl, jm), jnp.where(ge, jm, jh)

        n_iter = max(1, int(k_ref.shape[0] - 1).bit_length()) + 1
        jl, jh = lax.fori_loop(
            0, n_iter, body,
            (jnp.full((1, TQ), -1.0, jnp.float32), jnp.zeros((1, TQ), jnp.float32) + (q0 + TQ - 1).astype(jnp.float32)))
        return jnp.where(excess > 0.5, jh, jall)

    jstar = lax.cond(jnp.sum(excess) > 0.0, tie_search, lambda _: jall, 0)

    zq = jnp.zeros((), q_ref.dtype)
    for p in range(A_HEADS // 2):
        qp = q_ref[:, p * LANES:(p + 1) * LANES]
        qz_ref[p, 0:TQ, :] = jnp.where(first_half, qp, zq)
        qz_ref[p, TQ:2 * TQ, :] = jnp.where(first_half, zq, qp)
    m_ref[...] = jnp.full(m_ref.shape, NEG_BIG, jnp.float32)
    l_ref[...] = jnp.zeros(l_ref.shape, jnp.float32)
    acc_ref[...] = jnp.zeros(acc_ref.shape, jnp.float32)

    s_ref[nkc] = jnp.full((KC, TQ), -jnp.inf, jnp.float32)
    s_ref[nkc + 1] = jnp.full((KC, TQ), -jnp.inf, jnp.float32)
    last_chunk = k_ref.shape[0] // KC - 1

    def stage_a(c, buf):
        c = jnp.asarray(c, jnp.int32)
        r0 = pl.multiple_of(jnp.minimum(c, last_chunk) * KC, KC)
        sc = s_ref[c]
        kidx = kidx0 + (c * KC).astype(jnp.float32)
        bias = jnp.where(sc > thr, 0.0,
                         jnp.where(sc == thr, jnp.where(kidx <= jstar, 0.0, NEG_BIG), NEG_BIG))
        bias2 = jnp.concatenate([bias, bias], axis=1)
        for p in range(A_HEADS // 2):
            kc = k_ref[pl.ds(r0, KC), p * LANES:(p + 1) * LANES]
            s = lax.dot_general(kc, qz_ref[p], _NT, preferred_element_type=jnp.float32) + bias2
            sb_ref[buf, p] = s
            cm_ref[buf, p] = jnp.max(s, axis=0, keepdims=True)

    def stage_b(c, buf):
        cv = jnp.minimum(c, last_chunk)
        for p in range(A_HEADS // 2):
            vtc = vt_ref[cv, p * LANES:(p + 1) * LANES, :]
            m_old = m_ref[p]
            m_new = jnp.maximum(m_old, cm_ref[buf, p])
            alpha = jnp.exp(m_old - m_new)
            pe = jnp.exp(sb_ref[buf, p] - m_new)
            l_ref[p] = alpha * l_ref[p] + jnp.sum(pe, axis=0, keepdims=True)
            acc_ref[p] = alpha * acc_ref[p] + jnp.dot(vtc, pe.astype(jnp.bfloat16),
                                                      preferred_element_type=jnp.float32)
            m_ref[p] = m_new

    stage_a(0, 0)

    def attn_body(cc, carry):
        c = 2 * cc
        stage_a(c + 1, 1)
        stage_b(c, 0)
        stage_a(c + 2, 0)
        stage_b(c + 1, 1)
        return carry

    lax.fori_loop(0, (nkc + 1) // 2, attn_body, 0)

    top_rows = lax.broadcasted_iota(jnp.int32, (LANES, 1), 0) < HEAD_DIM
    for p in range(A_HEADS // 2):
        o = acc_ref[p] / l_ref[p]
        o_t = jnp.where(top_rows, o[:, 0:TQ], o[:, TQ:2 * TQ])
        o_ref[:, p * LANES:(p + 1) * LANES] = o_t.T.astype(o_ref.dtype)


def _dsa(p16, vt, p32, nbat, S):
    T = nbat * S
    TQ, KC = DSA_TQ, DSA_KC
    nq = S // TQ
    nc = S // KC
    n_keep = min(TOPK_MAX, S // 4)
    once = pl.Buffered(1)
    return pl.pallas_call(
        functools.partial(_dsa_kernel, n_keep=n_keep),
        grid=(nbat, nq),
        in_specs=[
            pl.BlockSpec((TQ, A_WIDTH), lambda b, i: (b * nq + i, P16_QA // A_WIDTH)),
            pl.BlockSpec((S, A_WIDTH), lambda b, i: (b, P16_KA // A_WIDTH), pipeline_mode=once),
            pl.BlockSpec((nc, A_WIDTH, KC), lambda b, i: (b, 0, 0), pipeline_mode=once),
            pl.BlockSpec((TQ, 2 * LANES), lambda b, i: (b * nq + i, P16_QI // (2 * LANES))),
            pl.BlockSpec((S, LANES), lambda b, i: (b, P16_KI // LANES), pipeline_mode=once),
            pl.BlockSpec((TQ, LANES), lambda b, i: (b * nq + i, P32_WI // LANES)),
        ],
        out_specs=pl.BlockSpec((TQ, A_WIDTH), lambda b, i: (b * nq + i, 0)),
        out_shape=jax.ShapeDtypeStruct((T, A_WIDTH), jnp.bfloat16),
        scratch_shapes=[
            pltpu.VMEM((nc + 2, KC, TQ), jnp.float32),
            pltpu.VMEM((A_HEADS // 2, 2 * TQ, LANES), jnp.bfloat16),
            pltpu.VMEM((A_HEADS // 2, 1, 2 * TQ), jnp.float32),
            pltpu.VMEM((A_HEADS // 2, 1, 2 * TQ), jnp.float32),
            pltpu.VMEM((A_HEADS // 2, LANES, 2 * TQ), jnp.float32),
            pltpu.VMEM((2, A_HEADS // 2, KC, 2 * TQ), jnp.float32),
            pltpu.VMEM((2, A_HEADS // 2, 1, 2 * TQ), jnp.float32),
        ],
        compiler_params=pltpu.CompilerParams(
            dimension_semantics=("arbitrary", "arbitrary"), vmem_limit_bytes=VMEM_LIMIT),
        name="dsa",
    )(p16, p16, vt, p16, p16, p32)


def _gla_kernel(q_ref, k_ref, v_ref, r_ref, gl_ref, wg_ref, bg_ref, gn_ref, o_ref, st_ref):
    C = GLA_C

    @pl.when(pl.program_id(1) == 0)
    def _():
        st_ref[...] = jnp.zeros(st_ref.shape, jnp.float32)

    row = lax.broadcasted_iota(jnp.int32, (C, C), 0)
    col = lax.broadcasted_iota(jnp.int32, (C, C), 1)
    tril = row >= col
    tril_f = jnp.where(tril, 1.0, 0.0)
    tril4 = jnp.concatenate([tril] * B_HEADS, axis=0)
    khead = lax.broadcasted_iota(jnp.int32, (1, B_KWIDTH), 1) // B_DK
    vhead = lax.broadcasted_iota(jnp.int32, (1, B_VWIDTH), 1) // B_DV
    st_rows = lax.broadcasted_iota(jnp.int32, (B_VWIDTH, B_KWIDTH), 0) // B_DV
    st_cols = lax.broadcasted_iota(jnp.int32, (B_VWIDTH, B_KWIDTH), 1) // B_DK
    same_head = st_rows == st_cols
    seg_r = lax.broadcasted_iota(jnp.int32, (B_VWIDTH, B_VWIDTH), 0) // B_DV
    seg_c = lax.broadcasted_iota(jnp.int32, (B_VWIDTH, B_VWIDTH), 1) // B_DV
    seg_mean = jnp.where(seg_r == seg_c, 1.0 / B_DV, 0.0)
    wg = wg_ref[...]
    bg = bg_ref[...]
    gn = gn_ref[...]

    def chunk(ci, carry):
        r0 = pl.multiple_of(ci * C, C)
        q = q_ref[pl.ds(r0, C), :]
        k = k_ref[pl.ds(r0, C), :]
        v = v_ref[pl.ds(r0, C), :].astype(jnp.bfloat16)
        z = jnp.dot(gl_ref[pl.ds(r0, C), :], wg, precision=HIGHEST, preferred_element_type=jnp.float32) + bg
        log_a = (jnp.minimum(z, 0.0) - jnp.log1p(jnp.exp(-jnp.abs(z)))) / GATE_TAU
        b = jnp.dot(tril_f, log_a, precision=HIGHEST, preferred_element_type=jnp.float32)
        b_last = b[C - 1:C, :]
        qt = (q * (B_DK ** -0.5)) * jnp.exp(b)
        kt = (k * jnp.exp(-b)).astype(jnp.bfloat16)
        kh = (k * jnp.exp(b_last - b)).astype(jnp.bfloat16)
        qz = jnp.concatenate([jnp.where(khead == h, qt, 0.0) for h in range(B_HEADS)], axis=0)
        att = lax.dot_general(qz.astype(jnp.bfloat16), kt, _NT, preferred_element_type=jnp.float32)
        att = jnp.where(tril4, att, 0.0).astype(jnp.bfloat16)
        res = jnp.dot(att, v, preferred_element_type=jnp.float32)
        st = st_ref[...]
        o = lax.dot_general(qt.astype(jnp.bfloat16), st.astype(jnp.bfloat16), _NT,
                            preferred_element_type=jnp.float32)
        for h in range(B_HEADS):
            o = o + jnp.where(vhead == h, res[h * C:(h + 1) * C], 0.0)
        upd = lax.dot_general(v, kh, _TN, preferred_element_type=jnp.float32)
        st_ref[...] = st * jnp.exp(b_last) + jnp.where(same_head, upd, 0.0)
        ms = jnp.dot(o * o, seg_mean, precision=HIGHEST, preferred_element_type=jnp.float32)
        y = o * lax.rsqrt(ms + EPS) * gn
        r = r_ref[pl.ds(r0, C), :]
        o_ref[pl.ds(r0, C), :] = (y * (r * jax.nn.sigmoid(r))).astype(o_ref.dtype)
        return carry

    lax.fori_loop(0, GLA_CT // C, chunk, 0)


def _gla(p32, wg_pad, bg, gn, nbat, S):
    T = nbat * S
    CT = GLA_CT
    nt = S // CT
    row = lambda b, i: b * nt + i
    return pl.pallas_call(
        _gla_kernel,
        grid=(nbat, nt),
        in_specs=[
            pl.BlockSpec((CT, B_KWIDTH), lambda b, i: (row(b, i), P32_QB // B_KWIDTH)),
            pl.BlockSpec((CT, B_KWIDTH), lambda b, i: (row(b, i), P32_KB // B_KWIDTH)),
            pl.BlockSpec((CT, B_VWIDTH), lambda b, i: (row(b, i), P32_VB // B_VWIDTH)),
            pl.BlockSpec((CT, B_VWIDTH), lambda b, i: (row(b, i), P32_RB // B_VWIDTH)),
            pl.BlockSpec((CT, LANES), lambda b, i: (row(b, i), P32_GL // LANES)),
            pl.BlockSpec((LANES, B_KWIDTH), lambda b, i: (0, 0)),
            pl.BlockSpec((1, B_KWIDTH), lambda b, i: (0, 0)),
            pl.BlockSpec((1, B_VWIDTH), lambda b, i: (0, 0)),
        ],
        out_specs=pl.BlockSpec((CT, B_VWIDTH), lambda b, i: (row(b, i), 0)),
        out_shape=jax.ShapeDtypeStruct((T, B_VWIDTH), jnp.bfloat16),
        scratch_shapes=[pltpu.VMEM((B_VWIDTH, B_KWIDTH), jnp.float32)],
        compiler_params=pltpu.CompilerParams(
            dimension_semantics=("arbitrary", "arbitrary"), vmem_limit_bytes=VMEM_LIMIT),
        name="gla",
    )(p32, p32, p32, p32, p32, wg_pad, bg, gn)


def _mix_kernel(x_ref, oa_ref, ob_ref, uc_ref, uh_ref, wo_ref, wp_ref, ps_ref, o_ref, ext_ref, *, seq):
    TM = MIX_TM
    t0 = (pl.program_id(0) * TM) % seq
    uc = uc_ref[...]
    @pl.when(t0 == 0)
    def _():
        ext_ref[0:POOL_HALO, :] = jnp.zeros((POOL_HALO, C_WIDTH), jnp.float32)

    @pl.when(t0 != 0)
    def _():
        ext_ref[0:POOL_HALO, :] = uh_ref[...]

    ext_ref[POOL_HALO:POOL_HALO + TM, :] = uc
    lane = lax.broadcasted_iota(jnp.int32, (1, C_WIDTH), 1)
    group = lane // C_GROUP_DIM
    wsum = jnp.zeros_like(uc)
    cum = uc
    for j in range(1, max(POOL_WINDOWS)):
        cum = cum + ext_ref[POOL_HALO - j:POOL_HALO - j + TM, :]
        if (j + 1) in POOL_WINDOWS:
            wsum = jnp.where(group == POOL_WINDOWS.index(j + 1), cum, wsum)
    win = jnp.zeros((1, C_WIDTH), jnp.float32)
    for g, w in enumerate(POOL_WINDOWS):
        win = jnp.where(group == g, float(w), win)
    tpos = (t0 + lax.broadcasted_iota(jnp.int32, (TM, 1), 0)).astype(jnp.float32)
    cnt = jnp.minimum(tpos + 1.0, win)
    pooled = wsum / cnt - uc
    y = jnp.dot(pooled.astype(jnp.bfloat16), wp_ref[...], preferred_element_type=jnp.float32) * ps_ref[...]
    mix = jnp.dot(oa_ref[...], wo_ref[0:A_WIDTH, :], preferred_element_type=jnp.float32)
    mix = mix + jnp.dot(ob_ref[...], wo_ref[A_WIDTH:A_WIDTH + B_VWIDTH, :], preferred_element_type=jnp.float32)
    mix = mix + jnp.dot(y.astype(jnp.bfloat16), wo_ref[A_WIDTH + B_VWIDTH:MIX_WIDTH, :],
                        preferred_element_type=jnp.float32)
    o_ref[...] = x_ref[...] + mix


def _mix(x2, o_a, o_b, p32, w_out, wp_bd, ps, S):
    T = x2.shape[0]
    TM = MIX_TM
    per = TM // POOL_HALO
    return pl.pallas_call(
        functools.partial(_mix_kernel, seq=S),
        grid=(T // TM,),
        in_specs=[
            pl.BlockSpec((TM, D_MODEL), lambda i: (i, 0)),
            pl.BlockSpec((TM, A_WIDTH), lambda i: (i, 0)),
            pl.BlockSpec((TM, B_VWIDTH), lambda i: (i, 0)),
            pl.BlockSpec((TM, C_WIDTH), lambda i: (i, P32_UC // C_WIDTH)),
            pl.BlockSpec((POOL_HALO, C_WIDTH), lambda i: (jnp.maximum(i * per - 1, 0), P32_UC // C_WIDTH)),
            pl.BlockSpec((MIX_WIDTH, D_MODEL), lambda i: (0, 0)),
            pl.BlockSpec((C_WIDTH, C_WIDTH), lambda i: (0, 0)),
            pl.BlockSpec((1, C_WIDTH), lambda i: (0, 0)),
        ],
        out_specs=pl.BlockSpec((TM, D_MODEL), lambda i: (i, 0)),
        out_shape=jax.ShapeDtypeStruct((T, D_MODEL), jnp.float32),
        scratch_shapes=[pltpu.VMEM((POOL_HALO + TM, C_WIDTH), jnp.float32)],
        compiler_params=pltpu.CompilerParams(
            dimension_semantics=("arbitrary",), vmem_limit_bytes=VMEM_LIMIT),
        name="mix",
    )(x2, o_a, o_b, p32, p32, w_out, wp_bd, ps)


def _ffn_kernel(x_ref, g_ref, wa_ref, wb_ref, cw_ref, cb_ref, wd_ref, fg_ref, o_ref,
                h_ref, acc_ref, ext_ref, carry_ref, *, seq, final):
    TM = FFN_TM
    i = pl.program_id(0)
    j = pl.program_id(1)
    t0 = (i * TM) % seq

    @pl.when(j == 0)
    def _():
        h_ref[...] = _rms(x_ref[...], g_ref[...]).astype(jnp.bfloat16)
        acc_ref[...] = jnp.zeros(acc_ref.shape, jnp.float32)

    h = h_ref[...]
    a = jnp.dot(h, wa_ref[...], preferred_element_type=jnp.float32)
    bgate = jnp.dot(h, wb_ref[...], preferred_element_type=jnp.float32)
    @pl.when(t0 == 0)
    def _():
        ext_ref[0:SUBLANES, :] = jnp.zeros((SUBLANES, ext_ref.shape[1]), jnp.float32)

    @pl.when(t0 != 0)
    def _():
        ext_ref[0:SUBLANES, :] = carry_ref[j]

    ext_ref[SUBLANES:SUBLANES + TM, :] = a
    carry_ref[j] = a[TM - SUBLANES:TM, :]
    conv = cb_ref[...] + a * cw_ref[CONV_WIDTH - 1:CONV_WIDTH, :]
    for d in range(1, CONV_WIDTH):
        conv = conv + ext_ref[SUBLANES - d:SUBLANES - d + TM, :] * cw_ref[CONV_WIDTH - 1 - d:CONV_WIDTH - d, :]
    gate = (conv * jax.nn.sigmoid(conv) * bgate).astype(jnp.bfloat16)
    acc_ref[...] += jnp.dot(gate, wd_ref[...], preferred_element_type=jnp.float32)

    @pl.when(j == pl.num_programs(1) - 1)
    def _():
        y = x_ref[...] + acc_ref[...]
        if final:
            y = _rms(y, fg_ref[...])
        o_ref[...] = y


def _ffn(x2, g, w_up, conv_w, conv_b, w_down, fg, S, final):
    T = x2.shape[0]
    TM, TF = FFN_TM, FFN_TF
    nj = D_FF // TF
    return pl.pallas_call(
        functools.partial(_ffn_kernel, seq=S, final=final),
        grid=(T // TM, nj),
        in_specs=[
            pl.BlockSpec((TM, D_MODEL), lambda i, j: (i, 0)),
            pl.BlockSpec((1, D_MODEL), lambda i, j: (0, 0)),
            pl.BlockSpec((D_MODEL, TF), lambda i, j: (0, j)),
            pl.BlockSpec((D_MODEL, TF), lambda i, j: (0, nj + j)),
            pl.BlockSpec((CONV_WIDTH, TF), lambda i, j: (0, j)),
            pl.BlockSpec((1, TF), lambda i, j: (0, j)),
            pl.BlockSpec((TF, D_MODEL), lambda i, j: (j, 0)),
            pl.BlockSpec((1, D_MODEL), lambda i, j: (0, 0)),
        ],
        out_specs=pl.BlockSpec((TM, D_MODEL), lambda i, j: (i, 0)),
        out_shape=jax.ShapeDtypeStruct((T, D_MODEL), jnp.float32),
        scratch_shapes=[
            pltpu.VMEM((TM, D_MODEL), jnp.bfloat16),
            pltpu.VMEM((TM, D_MODEL), jnp.float32),
            pltpu.VMEM((SUBLANES + TM, TF), jnp.float32),
            pltpu.VMEM((nj, SUBLANES, TF), jnp.float32),
        ],
        compiler_params=pltpu.CompilerParams(
            dimension_semantics=("arbitrary", "arbitrary"), vmem_limit_bytes=VMEM_LIMIT),
        name="ffn_final" if final else "ffn",
    )(x2, g, w_up, w_up, conv_w, conv_b, w_down, fg)


def _pad_cols(w, width):
    return jnp.pad(w, ((0, 0), (0, width - w.shape[1])))


def _pack_w_in(w):
    sizes = (A_WIDTH, A_WIDTH, A_WIDTH, IDX_HEADS * IDX_DIM, IDX_DIM, IDX_HEADS,
             B_KWIDTH, B_KWIDTH, B_VWIDTH, B_VWIDTH, GATE_RANK, C_WIDTH)
    parts, off = [], 0
    for n in sizes:
        parts.append(w[:, off:off + n])
        off += n
    qa, ka, va, qi, ki, wi, qb, kb, vb, rb, gl, uc = parts
    cols = [qa, ka, qi, ki, ki, vb, rb, uc, _pad_cols(wi, LANES), qb, kb, _pad_cols(gl, LANES)]
    return jnp.concatenate(cols, axis=1).astype(jnp.bfloat16), va.T.astype(jnp.bfloat16)


def _rope_table():
    lane = jnp.arange(LANES) % HEAD_DIM
    inv = ROPE_THETA ** (-jnp.arange(0, ROT_DIM, 2, dtype=jnp.float32) / ROT_DIM)
    half = ROT_DIM // 2
    inv_lane = jnp.where(lane < ROT_DIM, inv[lane % half], 0.0)
    lo = jnp.where(lane < half, -1.0, 0.0)
    hi = jnp.where((lane >= half) & (lane < ROT_DIM), 1.0, 0.0)
    tab = jnp.zeros((SUBLANES, LANES), jnp.float32)
    return tab.at[0].set(inv_lane).at[1].set(lo).at[2].set(hi)


def kernel(x, positions, norm1_g, w_in, w_gate_up, b_gate, gla_norm_g, w_pool, pool_scale, w_out, norm2_g, w_up, conv_w, conv_b, w_down, final_norm_g):
    nbat, S, _ = x.shape
    T = nbat * S
    depth = w_in.shape[0]
    assert S % DSA_KC == 0 and S % GLA_CT == 0 and S % MIX_TM == 0 and S % FFN_TM == 0 and PROJ_TM == DSA_KC
    x2 = x.reshape(T, D_MODEL)
    pos2 = positions.reshape(T, 1)
    tab = _rope_table()
    fg = final_norm_g.reshape(1, D_MODEL)
    bf = jnp.bfloat16
    for l in range(depth):
        w_pad, wvt = _pack_w_in(w_in[l])
        p16, vt, p32 = _proj(x2, pos2, norm1_g[l].reshape(1, D_MODEL), w_pad, wvt, tab)
        o_a = _dsa(p16, vt, p32, nbat, S)
        wg_pad = jnp.pad(w_gate_up[l], ((0, LANES - GATE_RANK), (0, 0)))
        o_b = _gla(p32, wg_pad, b_gate[l].reshape(1, B_KWIDTH), gla_norm_g[l].reshape(1, B_VWIDTH), nbat, S)
        wp_bd = jax.scipy.linalg.block_diag(*[w_pool[l, g] for g in range(C_GROUPS)]).astype(bf)
        x2 = _mix(x2, o_a, o_b, p32, w_out[l].astype(bf), wp_bd, pool_scale[l].reshape(1, C_WIDTH), S)
        x2 = _ffn(x2, norm2_g[l].reshape(1, D_MODEL), w_up[l].astype(bf), conv_w[l],
                  conv_b[l].reshape(1, D_FF), w_down[l].astype(bf), fg, S, final=(l == depth - 1))
    return x2.reshape(nbat, S, D_MODEL)
```

```python
import functools

import jax
import jax.numpy as jnp
from jax import lax
from jax.experimental import pallas as pl
from jax.experimental.pallas import tpu as pltpu

D_MODEL = 1024
A_HEADS = 8
HEAD_DIM = 64
A_WIDTH = A_HEADS * HEAD_DIM
IDX_HEADS = 4
IDX_DIM = 64
TOPK_MAX = 256
ATTN_SCALE = HEAD_DIM ** -0.5
IDX_SCALE = (IDX_DIM ** -0.5) * (IDX_HEADS ** -0.5)
ROPE_THETA = 500000.0
ROT_DIM = HEAD_DIM // 4
B_HEADS = 4
B_DK = 32
B_DV = 64
B_KWIDTH = B_HEADS * B_DK
B_VWIDTH = B_HEADS * B_DV
GATE_RANK = 16
GATE_TAU = 16.0
POOL_WINDOWS = (2, 4, 8, 16)
C_GROUPS = 4
C_GROUP_DIM = 64
C_WIDTH = C_GROUPS * C_GROUP_DIM
MIX_WIDTH = A_WIDTH + B_VWIDTH + C_WIDTH
D_FF = 2816
CONV_WIDTH = 3
EPS = 1e-6

LANES = 128
SUBLANES = 8
VMEM_LIMIT = 56 * 1024 * 1024

P16_QA, P16_KA, P16_QI, P16_KI = 0, 512, 1024, 1280
P16_WIDTH = 1408
PW_32 = P16_WIDTH
P32_VB, P32_RB, P32_UC, P32_WI, P32_QB, P32_KB, P32_GL = 0, 256, 512, 768, 896, 1024, 1152
P32_WIDTH = 1280

PROJ_TM = 512
DSA_TQ = 128
DSA_KC = 512
DSA_ACC = 32
BF16_ROWS = 16
VT_ROWS = LANES + BF16_ROWS
VT_WIDTH = (A_HEADS // 2) * VT_ROWS
LOG2E = 1.4426950408889634
DSA_CLS = 128
DSA_FIRST_STEPS = 7
DSA_ROUND_STEPS = 2
DSA_RANK_MAX = 8
DSA_MAX_ROUNDS = 96
GLA_CT = 512
GLA_C = 64
GLA_UNROLL = 8
MIX_TM = 512
POOL_HALO = 16
FFN_TM = 512
FFN_TF = 1408

NEG_BIG = -1e30
_NT = (((1,), (1,)), ((), ()))
_TN = (((0,), (0,)), ((), ()))


def _split2(x, keep_rest=False):
    hi = x.astype(jnp.bfloat16)
    rest = x - hi.astype(jnp.float32)
    return hi, (rest if keep_rest else rest.astype(jnp.bfloat16))


def _rms(x, g):
    ms = jnp.mean(x * x, axis=-1, keepdims=True)
    return x * lax.rsqrt(ms + EPS) * g


def _proj_kernel(x_ref, pos_ref, g_ref, w_ref, wvt_ref, tab_ref, o16_ref, vt_ref, o32_ref):
    h = _rms(x_ref[...], g_ref[...]).astype(jnp.bfloat16)
    ang = pos_ref[...].astype(jnp.float32) * tab_ref[0:1, :]
    cos = jnp.cos(ang)
    sin = jnp.sin(ang)
    sin_lo = sin * tab_ref[1:2, :]
    sin_hi = sin * tab_ref[2:3, :]

    def seg(a, b):
        return jnp.dot(h, w_ref[:, a:b], preferred_element_type=jnp.float32)

    def rope_store(p, col, scale=None):
        for g in range(p.shape[1] // LANES):
            xg = p[:, g * LANES:(g + 1) * LANES]
            half = ROT_DIM // 2
            r = xg * cos + pltpu.roll(xg, LANES - half, 1) * sin_lo + pltpu.roll(xg, half, 1) * sin_hi
            if scale is not None:
                r = r * scale
            o16_ref[:, col + g * LANES: col + (g + 1) * LANES] = r.astype(jnp.bfloat16)

    rope_store(seg(P16_QA, P16_KA), P16_QA, ATTN_SCALE * LOG2E)
    rope_store(seg(P16_KA, P16_QI), P16_KA)
    rope_store(seg(P16_QI, P16_KI), P16_QI)
    rope_store(seg(P16_KI, P16_WIDTH), P16_KI)
    vt = lax.dot_general(wvt_ref[...], h, _NT,
                         preferred_element_type=jnp.float32).astype(jnp.bfloat16)
    for p in range(A_HEADS // 2):
        vt_ref[0, p * VT_ROWS:p * VT_ROWS + LANES, :] = vt[p * LANES:(p + 1) * LANES, :]
        vt_ref[0, p * VT_ROWS + LANES:(p + 1) * VT_ROWS, :] = jnp.ones((BF16_ROWS, vt.shape[1]), jnp.bfloat16)
    base = PW_32
    o32_ref[:, P32_VB:P32_WI] = seg(base + P32_VB, base + P32_WI)
    o32_ref[:, P32_WI:P32_QB] = seg(base + P32_WI, base + P32_QB) * IDX_SCALE
    o32_ref[:, P32_QB:P32_WIDTH] = seg(base + P32_QB, base + P32_WIDTH)


def _proj(x2, pos2, g, w_pad, wvt, tab):
    T = x2.shape[0]
    tm = PROJ_TM
    return pl.pallas_call(
        _proj_kernel,
        grid=(T // tm,),
        in_specs=[
            pl.BlockSpec((tm, D_MODEL), lambda i: (i, 0)),
            pl.BlockSpec((tm, 1), lambda i: (i, 0)),
            pl.BlockSpec((1, D_MODEL), lambda i: (0, 0)),
            pl.BlockSpec((D_MODEL, PW_32 + P32_WIDTH), lambda i: (0, 0)),
            pl.BlockSpec((A_WIDTH, D_MODEL), lambda i: (0, 0)),
            pl.BlockSpec((SUBLANES, LANES), lambda i: (0, 0)),
        ],
        out_specs=[
            pl.BlockSpec((tm, P16_WIDTH), lambda i: (i, 0)),
            pl.BlockSpec((1, VT_WIDTH, tm), lambda i: (i, 0, 0)),
            pl.BlockSpec((tm, P32_WIDTH), lambda i: (i, 0)),
        ],
        out_shape=[
            jax.ShapeDtypeStruct((T, P16_WIDTH), jnp.bfloat16),
            jax.ShapeDtypeStruct((T // tm, VT_WIDTH, tm), jnp.bfloat16),
            jax.ShapeDtypeStruct((T, P32_WIDTH), jnp.float32),
        ],
        compiler_params=pltpu.CompilerParams(
            dimension_semantics=("arbitrary",), vmem_limit_bytes=VMEM_LIMIT),
        name="proj",
    )(x2, pos2, g, w_pad, wvt, tab)


def _dsa_kernel(q_ref, k_ref, vt_ref, qi_ref, ki_ref, wi_ref, o_ref,
                s_ref, lg_ref, qz_ref, m_ref, acc_ref, sb_ref, cm_ref, *, n_keep):
    TQ, KC = DSA_TQ, DSA_KC
    i = pl.program_id(1)
    q0 = i * TQ
    nkc = (q0 + TQ + KC - 1) // KC
    kf = jnp.float32(n_keep)

    lane = lax.broadcasted_iota(jnp.int32, (1, LANES), 1)
    first_half = lane < HEAD_DIM
    tpos = (q0 + lax.broadcasted_iota(jnp.int32, (1, TQ), 1)).astype(jnp.float32)
    kidx0 = lax.broadcasted_iota(jnp.int32, (KC, 1), 0).astype(jnp.float32)

    zero16 = jnp.zeros((), jnp.bfloat16)
    qi = qi_ref[...]
    qiz = jnp.concatenate(
        [jnp.where(first_half if h % 2 == 0 else ~first_half, qi[:, (h // 2) * LANES:(h // 2 + 1) * LANES], zero16)
         for h in range(IDX_HEADS)], axis=0)
    w_rows = wi_ref[...].T

    last_chunk = k_ref.shape[0] // KC - 1

    def score_logits(c, buf):
        c = jnp.asarray(c, jnp.int32)
        r0 = pl.multiple_of(jnp.minimum(c, last_chunk) * KC, KC)
        lg_ref[buf] = lax.dot_general(ki_ref[pl.ds(r0, KC), :], qiz, _NT,
                                      preferred_element_type=jnp.float32)

    def score_reduce(c, buf, carry):
        rmin, rmax = carry
        sc = w_rows[0:1, :] * jnp.maximum(lg_ref[buf, :, 0:TQ], 0.0)
        for h in range(1, IDX_HEADS):
            sc = sc + w_rows[h:h + 1, :] * jnp.maximum(lg_ref[buf, :, h * TQ:(h + 1) * TQ], 0.0)
        causal = (kidx0 + (c * KC).astype(jnp.float32)) <= tpos
        s_ref[c] = jnp.where(causal, sc, -jnp.inf)
        rmin = jnp.minimum(rmin, jnp.min(jnp.where(causal, sc, jnp.inf), axis=0, keepdims=True))
        rmax = jnp.maximum(rmax, jnp.max(jnp.where(causal, sc, -jnp.inf), axis=0, keepdims=True))
        return rmin, rmax

    def score_body(cc, carry):
        c = 2 * cc
        score_logits(c + 1, 1)
        carry = score_reduce(c, 0, carry)
        score_logits(c + 2, 0)
        return score_reduce(c + 1, 1, carry)

    score_logits(0, 0)
    rmin, rmax = lax.fori_loop(
        0, (nkc + 1) // 2, score_body,
        (jnp.full((1, TQ), jnp.inf, jnp.float32), jnp.full((1, TQ), -jnp.inf, jnp.float32)))

    def fold(m):
        return jnp.sum(m.reshape(KC // DSA_ACC, DSA_ACC, TQ), axis=0)

    def count_gt(x):
        def body(c, acc):
            return acc + fold(jnp.where(s_ref[c] > x, 1.0, 0.0))
        acc = lax.fori_loop(0, nkc, body, jnp.zeros((DSA_ACC, TQ), jnp.float32))
        return jnp.sum(acc, axis=0, keepdims=True)

    def count_eq(x):
        def body(c, acc):
            return acc + fold(jnp.where(s_ref[c] == x, 1.0, 0.0))
        acc = lax.fori_loop(0, nkc, body, jnp.zeros((DSA_ACC, TQ), jnp.float32))
        return jnp.sum(acc, axis=0, keepdims=True)

    def class_min_above(x):
        def body(c, acc):
            blk = s_ref[c]
            m = jnp.where(blk > x, blk, jnp.inf)
            return jnp.minimum(acc, jnp.min(m.reshape(KC // DSA_CLS, DSA_CLS, TQ), axis=0))
        return lax.fori_loop(0, nkc, body, jnp.full((DSA_CLS, TQ), jnp.inf, jnp.float32))

    n_causal = tpos + 1.0
    all_kept = n_causal <= kf
    zeros_row = jnp.zeros((1, TQ), jnp.float32)
    log_target = jnp.log(kf + 0.5)

    def yval(cnt):
        return jnp.log(jnp.maximum(cnt, 0.5)) - log_target

    c0 = count_gt(zeros_row)
    e0 = count_eq(zeros_row)
    at_zero = jnp.logical_and(jnp.logical_and(c0 < kf, c0 + e0 >= kf), jnp.logical_not(all_kept))
    done0 = jnp.where(jnp.logical_or(all_kept, at_zero), 1.0, 0.0)
    thr0 = jnp.where(at_zero, 0.0, -jnp.inf)
    cgt0 = jnp.where(at_zero, c0, 0.0)
    ties0 = jnp.where(at_zero, e0, 0.0)
    above = c0 >= kf
    below = jnp.logical_and(jnp.logical_not(above), rmax > 0.0)
    lo0 = jnp.where(above, 0.0, -(2.0 * jnp.abs(rmin) + 1.0))
    clo0 = jnp.where(above, c0, n_causal)
    hi0 = jnp.where(below, 0.0, rmax)
    chi0 = jnp.where(below, c0 + e0, 0.0)

    def narrow(nst):
        lo, hi, clo, chi, ylo, yhi, side = nst
        frac = jnp.minimum(jnp.maximum(ylo / jnp.maximum(ylo - yhi, 1e-9), 0.0), 1.0)
        mid = jnp.minimum(jnp.maximum(lo + frac * (hi - lo), lo), hi)
        c = count_gt(mid)
        ge = c >= kf
        y = yval(c)
        yhi = jnp.where(jnp.logical_and(ge, side > 0.5), 0.5 * yhi, yhi)
        ylo = jnp.where(jnp.logical_and(jnp.logical_not(ge), side < -0.5), 0.5 * ylo, ylo)
        return (jnp.where(ge, mid, lo), jnp.where(ge, hi, mid), jnp.where(ge, c, clo), jnp.where(ge, chi, c),
                jnp.where(ge, y, ylo), jnp.where(ge, yhi, y), jnp.where(ge, 1.0, -1.0))

    nst = (lo0, hi0, clo0, chi0, yval(clo0), yval(chi0), zeros_row)
    nst = lax.fori_loop(0, DSA_FIRST_STEPS - DSA_ROUND_STEPS, lambda _, s: narrow(s), nst)

    def round_cond(st):
        return jnp.logical_and(st[0] < DSA_MAX_ROUNDS, st[1] > 0.0)

    def round_body(st):
        it, _, nst, thr, cgt, ties, done = st
        for _ in range(DSA_ROUND_STEPS):
            nst = narrow(nst)
        lo, hi, clo, chi, ylo, yhi, side = nst
        pm = class_min_above(lo)
        rank = clo - kf + 1.0
        cand = jnp.min(pm, axis=0, keepdims=True)
        for j in range(2, DSA_RANK_MAX + 1):
            pm = jnp.where(pm <= cand, jnp.inf, pm)
            nxt = jnp.min(pm, axis=0, keepdims=True)
            cand = jnp.where(jnp.logical_and(rank >= j, nxt < jnp.inf), nxt, cand)
        c2 = count_gt(cand)
        e2 = count_eq(cand)
        ok = jnp.logical_and(c2 < kf, c2 + e2 >= kf)
        newly = jnp.logical_and(ok, done < 0.5)
        thr = jnp.where(newly, cand, thr)
        cgt = jnp.where(newly, c2, cgt)
        ties = jnp.where(newly, e2, ties)
        done = jnp.where(ok, 1.0, done)
        up = c2 >= kf
        dn = c2 + e2 < kf
        nst = (jnp.where(up, cand, lo), jnp.where(dn, cand, hi), jnp.where(up, c2, clo),
               jnp.where(dn, c2 + e2, chi), jnp.where(up, yval(c2), ylo), jnp.where(dn, yval(c2 + e2), yhi), side)
        return (it + 1, jnp.sum(1.0 - done), nst, thr, cgt, ties, done)

    st = lax.while_loop(round_cond, round_body,
                        (jnp.int32(0), jnp.sum(1.0 - done0), nst, thr0, cgt0, ties0, done0))
    thr, cgt, ties = st[3], st[4], st[5]

    need = kf - cgt
    excess = jnp.where(jnp.logical_and(ties > need, jnp.logical_not(all_kept)), 1.0, 0.0)
    jall = jnp.where(all_kept, -1.0, jnp.float32(1e9))

    def tie_search(_):
        incl = (lax.broadcasted_iota(jnp.int32, (KC, KC), 0) >= lax.broadcasted_iota(jnp.int32, (KC, KC), 1))
        tri = jnp.where(incl, 1.0, 0.0).astype(jnp.bfloat16)

        def body(c, carry):
            run, jlast = carry
            tied = s_ref[c] == thr
            pref = jnp.dot(tri, jnp.where(tied, 1.0, 0.0).astype(jnp.bfloat16),
                           preferred_element_type=jnp.float32)
            kidx = kidx0 + (c * KC).astype(jnp.float32)
            kept = jnp.where(tied, jnp.where(pref + run <= need, kidx, -1.0), -1.0)
            return run + pref[KC - 1:KC, :], jnp.maximum(jlast, jnp.max(kept, axis=0, keepdims=True))

        _, jlast = lax.fori_loop(0, nkc, body, (zeros_row, jnp.full((1, TQ), -1.0, jnp.float32)))
        return jnp.where(excess > 0.5, jlast, jall)

    jstar = lax.cond(jnp.sum(excess) > 0.0, tie_search, lambda _: jall, 0)

    zq = jnp.zeros((), q_ref.dtype)
    for p in range(A_HEADS // 2):
        qp = q_ref[:, p * LANES:(p + 1) * LANES]
        qz_ref[p, 0:TQ, :] = jnp.where(first_half, qp, zq)
        qz_ref[p, TQ:2 * TQ, :] = jnp.where(first_half, zq, qp)
    m_ref[...] = jnp.full(m_ref.shape, NEG_BIG, jnp.float32)
    acc_ref[...] = jnp.zeros(acc_ref.shape, jnp.float32)

    s_ref[nkc] = jnp.full((KC, TQ), -jnp.inf, jnp.float32)
    s_ref[nkc + 1] = jnp.full((KC, TQ), -jnp.inf, jnp.float32)

    def stage_a(c, buf):
        c = jnp.asarray(c, jnp.int32)
        r0 = pl.multiple_of(jnp.minimum(c, last_chunk) * KC, KC)
        sc = s_ref[c]
        kidx = kidx0 + (c * KC).astype(jnp.float32)
        bias = jnp.where(sc > thr, 0.0,
                         jnp.where(sc == thr, jnp.where(kidx <= jstar, 0.0, NEG_BIG), NEG_BIG))
        bias2 = jnp.concatenate([bias, bias], axis=1)
        for p in range(A_HEADS // 2):
            kc = k_ref[pl.ds(r0, KC), p * LANES:(p + 1) * LANES]
            s = lax.dot_general(kc, qz_ref[p], _NT, preferred_element_type=jnp.float32) + bias2
            sb_ref[buf, p] = s
            cm_ref[buf, p] = jnp.max(s, axis=0, keepdims=True)

    def stage_b(c, buf):
        cv = jnp.minimum(c, last_chunk)
        for p in range(A_HEADS // 2):
            vtc = vt_ref[cv, p * VT_ROWS:(p + 1) * VT_ROWS, :]
            m_old = m_ref[p]
            m_new = jnp.maximum(m_old, cm_ref[buf, p])
            alpha = jnp.exp2(m_old - m_new)
            pe = jnp.exp2((sb_ref[buf, p] - m_new).astype(jnp.bfloat16))
            acc_ref[p] = alpha * acc_ref[p] + jnp.dot(vtc, pe, preferred_element_type=jnp.float32)
            m_ref[p] = m_new

    stage_a(0, 0)

    def attn_body(cc, carry):
        c = 2 * cc
        stage_a(c + 1, 1)
        stage_b(c, 0)
        stage_a(c + 2, 0)
        stage_b(c + 1, 1)
        return carry

    lax.fori_loop(0, (nkc + 1) // 2, attn_body, 0)

    top_rows = lax.broadcasted_iota(jnp.int32, (LANES, 1), 0) < HEAD_DIM
    for p in range(A_HEADS // 2):
        acc = acc_ref[p]
        o = acc[0:LANES] / acc[LANES:LANES + 1]
        o_t = jnp.where(top_rows, o[:, 0:TQ], o[:, TQ:2 * TQ])
        o_ref[:, p * LANES:(p + 1) * LANES] = o_t.T.astype(o_ref.dtype)


def _dsa(p16, vt, p32, nbat, S):
    T = nbat * S
    TQ, KC = DSA_TQ, DSA_KC
    nq = S // TQ
    nc = S // KC
    n_keep = min(TOPK_MAX, S // 4)
    once = pl.Buffered(1)
    return pl.pallas_call(
        functools.partial(_dsa_kernel, n_keep=n_keep),
        grid=(nbat, nq),
        in_specs=[
            pl.BlockSpec((TQ, A_WIDTH), lambda b, i: (b * nq + i, P16_QA // A_WIDTH)),
            pl.BlockSpec((S, A_WIDTH), lambda b, i: (b, P16_KA // A_WIDTH), pipeline_mode=once),
            pl.BlockSpec((nc, VT_WIDTH, KC), lambda b, i: (b, 0, 0), pipeline_mode=once),
            pl.BlockSpec((TQ, 2 * LANES), lambda b, i: (b * nq + i, P16_QI // (2 * LANES))),
            pl.BlockSpec((S, LANES), lambda b, i: (b, P16_KI // LANES), pipeline_mode=once),
            pl.BlockSpec((TQ, LANES), lambda b, i: (b * nq + i, P32_WI // LANES)),
        ],
        out_specs=pl.BlockSpec((TQ, A_WIDTH), lambda b, i: (b * nq + i, 0)),
        out_shape=jax.ShapeDtypeStruct((T, A_WIDTH), jnp.bfloat16),
        scratch_shapes=[
            pltpu.VMEM((nc + 2, KC, TQ), jnp.float32),
            pltpu.VMEM((2, KC, IDX_HEADS * TQ), jnp.float32),
            pltpu.VMEM((A_HEADS // 2, 2 * TQ, LANES), jnp.bfloat16),
            pltpu.VMEM((A_HEADS // 2, 1, 2 * TQ), jnp.float32),
            pltpu.VMEM((A_HEADS // 2, VT_ROWS, 2 * TQ), jnp.float32),
            pltpu.VMEM((2, A_HEADS // 2, KC, 2 * TQ), jnp.float32),
            pltpu.VMEM((2, A_HEADS // 2, 1, 2 * TQ), jnp.float32),
        ],
        compiler_params=pltpu.CompilerParams(
            dimension_semantics=("arbitrary", "arbitrary"), vmem_limit_bytes=VMEM_LIMIT),
        name="dsa",
    )(p16, p16, vt, p16, p16, p32)


def _gla_kernel(q_ref, k_ref, v_ref, r_ref, gl_ref, wg_ref, bg_ref, gn_ref, o_ref, st_ref):
    C = GLA_C

    @pl.when(pl.program_id(1) == 0)
    def _():
        st_ref[...] = jnp.zeros(st_ref.shape, jnp.float32)

    row = lax.broadcasted_iota(jnp.int32, (C, C), 0)
    col = lax.broadcasted_iota(jnp.int32, (C, C), 1)
    tril = row >= col
    tril_b = jnp.where(tril, 1.0, 0.0).astype(jnp.bfloat16)
    tril4 = jnp.concatenate([tril] * B_HEADS, axis=0)
    khead = lax.broadcasted_iota(jnp.int32, (1, B_KWIDTH), 1) // B_DK
    vhead = lax.broadcasted_iota(jnp.int32, (1, B_VWIDTH), 1) // B_DV
    st_rows = lax.broadcasted_iota(jnp.int32, (B_VWIDTH, B_KWIDTH), 0) // B_DV
    st_cols = lax.broadcasted_iota(jnp.int32, (B_VWIDTH, B_KWIDTH), 1) // B_DK
    same_head = st_rows == st_cols
    seg_r = lax.broadcasted_iota(jnp.int32, (B_VWIDTH, B_VWIDTH), 0) // B_DV
    seg_c = lax.broadcasted_iota(jnp.int32, (B_VWIDTH, B_VWIDTH), 1) // B_DV
    seg_mean = jnp.where(seg_r == seg_c, 1.0 / B_DV, 0.0).astype(jnp.bfloat16)
    wg_hi, wg_lo = _split2(wg_ref[...])
    bg = bg_ref[...]
    gn = gn_ref[...]

    def chunk(ci, st):
        r0 = pl.multiple_of(ci * C, C)
        q = q_ref[pl.ds(r0, C), :]
        k = k_ref[pl.ds(r0, C), :]
        v = v_ref[pl.ds(r0, C), :].astype(jnp.bfloat16)
        g_hi, g_lo = _split2(gl_ref[pl.ds(r0, C), :])
        zz = jnp.dot(jnp.concatenate([g_hi, g_lo], axis=0), wg_hi, preferred_element_type=jnp.float32)
        z = zz[0:C] + zz[C:2 * C] + jnp.dot(g_hi, wg_lo, preferred_element_type=jnp.float32) + bg
        log_a = (jnp.minimum(z, 0.0) - jnp.log1p(jnp.exp(-jnp.abs(z)))) / GATE_TAU
        a_hi, a_rest = _split2(log_a, keep_rest=True)
        a_mid, a_lo = _split2(a_rest)
        bb = jnp.dot(tril_b, jnp.concatenate([a_hi, a_mid, a_lo], axis=1), preferred_element_type=jnp.float32)
        b = bb[:, 0:B_KWIDTH] + bb[:, B_KWIDTH:2 * B_KWIDTH] + bb[:, 2 * B_KWIDTH:3 * B_KWIDTH]
        b_last = b[C - 1:C, :]
        qt = (q * (B_DK ** -0.5)) * jnp.exp(b)
        kt = (k * jnp.exp(-b)).astype(jnp.bfloat16)
        kh = (k * jnp.exp(b_last - b)).astype(jnp.bfloat16)
        qz = jnp.concatenate([jnp.where(khead == h, qt, 0.0) for h in range(B_HEADS)], axis=0)
        att = lax.dot_general(qz.astype(jnp.bfloat16), kt, _NT, preferred_element_type=jnp.float32)
        att = jnp.where(tril4, att, 0.0).astype(jnp.bfloat16)
        res = jnp.dot(att, v, preferred_element_type=jnp.float32)
        o = lax.dot_general(qt.astype(jnp.bfloat16), st.astype(jnp.bfloat16), _NT,
                            preferred_element_type=jnp.float32)
        for h in range(B_HEADS):
            o = o + jnp.where(vhead == h, res[h * C:(h + 1) * C], 0.0)
        upd = lax.dot_general(v, kh, _TN, preferred_element_type=jnp.float32)
        st_new = st * jnp.exp(b_last) + jnp.where(same_head, upd, 0.0)
        o2_hi, o2_lo = _split2(o * o)
        mm = jnp.dot(jnp.concatenate([o2_hi, o2_lo], axis=0), seg_mean, preferred_element_type=jnp.float32)
        ms = mm[0:C] + mm[C:2 * C]
        y = o * lax.rsqrt(ms + EPS) * gn
        r = r_ref[pl.ds(r0, C), :]
        o_ref[pl.ds(r0, C), :] = (y * (r * jax.nn.sigmoid(r))).astype(o_ref.dtype)
        return st_new

    st_ref[...] = lax.fori_loop(0, GLA_CT // C, chunk, st_ref[...], unroll=GLA_UNROLL)


def _gla(p32, wg_pad, bg, gn, nbat, S):
    T = nbat * S
    CT = GLA_CT
    nt = S // CT
    row = lambda b, i: b * nt + i
    return pl.pallas_call(
        _gla_kernel,
        grid=(nbat, nt),
        in_specs=[
            pl.BlockSpec((CT, B_KWIDTH), lambda b, i: (row(b, i), P32_QB // B_KWIDTH)),
            pl.BlockSpec((CT, B_KWIDTH), lambda b, i: (row(b, i), P32_KB // B_KWIDTH)),
            pl.BlockSpec((CT, B_VWIDTH), lambda b, i: (row(b, i), P32_VB // B_VWIDTH)),
            pl.BlockSpec((CT, B_VWIDTH), lambda b, i: (row(b, i), P32_RB // B_VWIDTH)),
            pl.BlockSpec((CT, LANES), lambda b, i: (row(b, i), P32_GL // LANES)),
            pl.BlockSpec((LANES, B_KWIDTH), lambda b, i: (0, 0)),
            pl.BlockSpec((1, B_KWIDTH), lambda b, i: (0, 0)),
            pl.BlockSpec((1, B_VWIDTH), lambda b, i: (0, 0)),
        ],
        out_specs=pl.BlockSpec((CT, B_VWIDTH), lambda b, i: (row(b, i), 0)),
        out_shape=jax.ShapeDtypeStruct((T, B_VWIDTH), jnp.bfloat16),
        scratch_shapes=[pltpu.VMEM((B_VWIDTH, B_KWIDTH), jnp.float32)],
        compiler_params=pltpu.CompilerParams(
            dimension_semantics=("arbitrary", "arbitrary"), vmem_limit_bytes=VMEM_LIMIT),
        name="gla",
    )(p32, p32, p32, p32, p32, wg_pad, bg, gn)


def _mix_kernel(x_ref, oa_ref, ob_ref, uc_ref, uh_ref, wo_ref, wp_ref, ps_ref, o_ref, ext_ref, *, seq):
    TM = MIX_TM
    t0 = (pl.program_id(0) * TM) % seq
    uc = uc_ref[...]
    @pl.when(t0 == 0)
    def _():
        ext_ref[0:POOL_HALO, :] = jnp.zeros((POOL_HALO, C_WIDTH), jnp.float32)

    @pl.when(t0 != 0)
    def _():
        ext_ref[0:POOL_HALO, :] = uh_ref[...]

    ext_ref[POOL_HALO:POOL_HALO + TM, :] = uc
    lane = lax.broadcasted_iota(jnp.int32, (1, C_WIDTH), 1)
    group = lane // C_GROUP_DIM
    wsum = jnp.zeros_like(uc)
    cum = uc
    for j in range(1, max(POOL_WINDOWS)):
        cum = cum + ext_ref[POOL_HALO - j:POOL_HALO - j + TM, :]
        if (j + 1) in POOL_WINDOWS:
            wsum = jnp.where(group == POOL_WINDOWS.index(j + 1), cum, wsum)
    win = jnp.zeros((1, C_WIDTH), jnp.float32)
    for g, w in enumerate(POOL_WINDOWS):
        win = jnp.where(group == g, float(w), win)
    tpos = (t0 + lax.broadcasted_iota(jnp.int32, (TM, 1), 0)).astype(jnp.float32)
    cnt = jnp.minimum(tpos + 1.0, win)
    pooled = wsum / cnt - uc
    y = jnp.dot(pooled.astype(jnp.bfloat16), wp_ref[...], preferred_element_type=jnp.float32) * ps_ref[...]
    mix = jnp.dot(oa_ref[...], wo_ref[0:A_WIDTH, :], preferred_element_type=jnp.float32)
    mix = mix + jnp.dot(ob_ref[...], wo_ref[A_WIDTH:A_WIDTH + B_VWIDTH, :], preferred_element_type=jnp.float32)
    mix = mix + jnp.dot(y.astype(jnp.bfloat16), wo_ref[A_WIDTH + B_VWIDTH:MIX_WIDTH, :],
                        preferred_element_type=jnp.float32)
    o_ref[...] = x_ref[...] + mix


def _mix(x2, o_a, o_b, p32, w_out, wp_bd, ps, S):
    T = x2.shape[0]
    TM = MIX_TM
    per = TM // POOL_HALO
    return pl.pallas_call(
        functools.partial(_mix_kernel, seq=S),
        grid=(T // TM,),
        in_specs=[
            pl.BlockSpec((TM, D_MODEL), lambda i: (i, 0)),
            pl.BlockSpec((TM, A_WIDTH), lambda i: (i, 0)),
            pl.BlockSpec((TM, B_VWIDTH), lambda i: (i, 0)),
            pl.BlockSpec((TM, C_WIDTH), lambda i: (i, P32_UC // C_WIDTH)),
            pl.BlockSpec((POOL_HALO, C_WIDTH), lambda i: (jnp.maximum(i * per - 1, 0), P32_UC // C_WIDTH)),
            pl.BlockSpec((MIX_WIDTH, D_MODEL), lambda i: (0, 0)),
            pl.BlockSpec((C_WIDTH, C_WIDTH), lambda i: (0, 0)),
            pl.BlockSpec((1, C_WIDTH), lambda i: (0, 0)),
        ],
        out_specs=pl.BlockSpec((TM, D_MODEL), lambda i: (i, 0)),
        out_shape=jax.ShapeDtypeStruct((T, D_MODEL), jnp.float32),
        scratch_shapes=[pltpu.VMEM((POOL_HALO + TM, C_WIDTH), jnp.float32)],
        compiler_params=pltpu.CompilerParams(
            dimension_semantics=("arbitrary",), vmem_limit_bytes=VMEM_LIMIT),
        name="mix",
    )(x2, o_a, o_b, p32, p32, w_out, wp_bd, ps)


def _ffn_kernel(x_ref, g_ref, wa_ref, wb_ref, cw_ref, cb_ref, wd_ref, fg_ref, o_ref,
                h_ref, acc_ref, ext_ref, carry_ref, *, seq, final):
    TM = FFN_TM
    i = pl.program_id(0)
    j = pl.program_id(1)
    t0 = (i * TM) % seq

    @pl.when(j == 0)
    def _():
        h_ref[...] = _rms(x_ref[...], g_ref[...]).astype(jnp.bfloat16)
        acc_ref[...] = jnp.zeros(acc_ref.shape, jnp.float32)

    h = h_ref[...]
    a = jnp.dot(h, wa_ref[...], preferred_element_type=jnp.float32)
    bgate = jnp.dot(h, wb_ref[...], preferred_element_type=jnp.float32)
    @pl.when(t0 == 0)
    def _():
        ext_ref[0:SUBLANES, :] = jnp.zeros((SUBLANES, ext_ref.shape[1]), jnp.float32)

    @pl.when(t0 != 0)
    def _():
        ext_ref[0:SUBLANES, :] = carry_ref[j]

    ext_ref[SUBLANES:SUBLANES + TM, :] = a
    carry_ref[j] = a[TM - SUBLANES:TM, :]
    conv = cb_ref[...] + a * cw_ref[CONV_WIDTH - 1:CONV_WIDTH, :]
    for d in range(1, CONV_WIDTH):
        conv = conv + ext_ref[SUBLANES - d:SUBLANES - d + TM, :] * cw_ref[CONV_WIDTH - 1 - d:CONV_WIDTH - d, :]
    gate = (conv * jax.nn.sigmoid(conv) * bgate).astype(jnp.bfloat16)
    acc_ref[...] += jnp.dot(gate, wd_ref[...], preferred_element_type=jnp.float32)

    @pl.when(j == pl.num_programs(1) - 1)
    def _():
        y = x_ref[...] + acc_ref[...]
        if final:
            y = _rms(y, fg_ref[...])
        o_ref[...] = y


def _ffn(x2, g, w_up, conv_w, conv_b, w_down, fg, S, final):
    T = x2.shape[0]
    TM, TF = FFN_TM, FFN_TF
    nj = D_FF // TF
    return pl.pallas_call(
        functools.partial(_ffn_kernel, seq=S, final=final),
        grid=(T // TM, nj),
        in_specs=[
            pl.BlockSpec((TM, D_MODEL), lambda i, j: (i, 0)),
            pl.BlockSpec((1, D_MODEL), lambda i, j: (0, 0)),
            pl.BlockSpec((D_MODEL, TF), lambda i, j: (0, j)),
            pl.BlockSpec((D_MODEL, TF), lambda i, j: (0, nj + j)),
            pl.BlockSpec((CONV_WIDTH, TF), lambda i, j: (0, j)),
            pl.BlockSpec((1, TF), lambda i, j: (0, j)),
            pl.BlockSpec((TF, D_MODEL), lambda i, j: (j, 0)),
            pl.BlockSpec((1, D_MODEL), lambda i, j: (0, 0)),
        ],
        out_specs=pl.BlockSpec((TM, D_MODEL), lambda i, j: (i, 0)),
        out_shape=jax.ShapeDtypeStruct((T, D_MODEL), jnp.float32),
        scratch_shapes=[
            pltpu.VMEM((TM, D_MODEL), jnp.bfloat16),
            pltpu.VMEM((TM, D_MODEL), jnp.float32),
            pltpu.VMEM((SUBLANES + TM, TF), jnp.float32),
            pltpu.VMEM((nj, SUBLANES, TF), jnp.float32),
        ],
        compiler_params=pltpu.CompilerParams(
            dimension_semantics=("arbitrary", "arbitrary"), vmem_limit_bytes=VMEM_LIMIT),
        name="ffn_final" if final else "ffn",
    )(x2, g, w_up, w_up, conv_w, conv_b, w_down, fg)


def _pad_cols(w, width):
    return jnp.pad(w, ((0, 0), (0, width - w.shape[1])))


def _pack_w_in(w):
    sizes = (A_WIDTH, A_WIDTH, A_WIDTH, IDX_HEADS * IDX_DIM, IDX_DIM, IDX_HEADS,
             B_KWIDTH, B_KWIDTH, B_VWIDTH, B_VWIDTH, GATE_RANK, C_WIDTH)
    parts, off = [], 0
    for n in sizes:
        parts.append(w[:, off:off + n])
        off += n
    qa, ka, va, qi, ki, wi, qb, kb, vb, rb, gl, uc = parts
    cols = [qa, ka, qi, ki, ki, vb, rb, uc, _pad_cols(wi, LANES), qb, kb, _pad_cols(gl, LANES)]
    return jnp.concatenate(cols, axis=1).astype(jnp.bfloat16), va.T.astype(jnp.bfloat16)


def _rope_table():
    lane = jnp.arange(LANES) % HEAD_DIM
    inv = ROPE_THETA ** (-jnp.arange(0, ROT_DIM, 2, dtype=jnp.float32) / ROT_DIM)
    half = ROT_DIM // 2
    inv_lane = jnp.where(lane < ROT_DIM, inv[lane % half], 0.0)
    lo = jnp.where(lane < half, -1.0, 0.0)
    hi = jnp.where((lane >= half) & (lane < ROT_DIM), 1.0, 0.0)
    tab = jnp.zeros((SUBLANES, LANES), jnp.float32)
    return tab.at[0].set(inv_lane).at[1].set(lo).at[2].set(hi)


def kernel(x, positions, norm1_g, w_in, w_gate_up, b_gate, gla_norm_g, w_pool, pool_scale, w_out, norm2_g, w_up, conv_w, conv_b, w_down, final_norm_g):
    nbat, S, _ = x.shape
    T = nbat * S
    depth = w_in.shape[0]
    assert S % DSA_KC == 0 and S % GLA_CT == 0 and S % MIX_TM == 0 and S % FFN_TM == 0 and PROJ_TM == DSA_KC
    x2 = x.reshape(T, D_MODEL)
    pos2 = positions.reshape(T, 1)
    tab = _rope_table()
    fg = final_norm_g.reshape(1, D_MODEL)
    bf = jnp.bfloat16
    for l in range(depth):
        w_pad, wvt = _pack_w_in(w_in[l])
        p16, vt, p32 = _proj(x2, pos2, norm1_g[l].reshape(1, D_MODEL), w_pad, wvt, tab)
        o_a = _dsa(p16, vt, p32, nbat, S)
        wg_pad = jnp.pad(w_gate_up[l], ((0, LANES - GATE_RANK), (0, 0)))
        o_b = _gla(p32, wg_pad, b_gate[l].reshape(1, B_KWIDTH), gla_norm_g[l].reshape(1, B_VWIDTH), nbat, S)
        wp_bd = jax.scipy.linalg.block_diag(*[w_pool[l, g] for g in range(C_GROUPS)]).astype(bf)
        x2 = _mix(x2, o_a, o_b, p32, w_out[l].astype(bf), wp_bd, pool_scale[l].reshape(1, C_WIDTH), S)
        x2 = _ffn(x2, norm2_g[l].reshape(1, D_MODEL), w_up[l].astype(bf), conv_w[l],
                  conv_b[l].reshape(1, D_FF), w_down[l].astype(bf), fg, S, final=(l == depth - 1))
    return x2.reshape(nbat, S, D_MODEL)
```

```python
import functools

import jax
import jax.numpy as jnp
from jax import lax
from jax.experimental import pallas as pl
from jax.experimental.pallas import tpu as pltpu

D_MODEL = 1024
A_HEADS = 8
HEAD_DIM = 64
A_WIDTH = A_HEADS * HEAD_DIM
IDX_HEADS = 4
IDX_DIM = 64
TOPK_MAX = 256
ATTN_SCALE = HEAD_DIM ** -0.5
IDX_SCALE = (IDX_DIM ** -0.5) * (IDX_HEADS ** -0.5)
ROPE_THETA = 500000.0
ROT_DIM = HEAD_DIM // 4
B_HEADS = 4
B_DK = 32
B_DV = 64
B_KWIDTH = B_HEADS * B_DK
B_VWIDTH = B_HEADS * B_DV
GATE_RANK = 16
GATE_TAU = 16.0
POOL_WINDOWS = (2, 4, 8, 16)
C_GROUPS = 4
C_GROUP_DIM = 64
C_WIDTH = C_GROUPS * C_GROUP_DIM
MIX_WIDTH = A_WIDTH + B_VWIDTH + C_WIDTH
D_FF = 2816
CONV_WIDTH = 3
EPS = 1e-6

LANES = 128
SUBLANES = 8
VMEM_LIMIT = 56 * 1024 * 1024

P16_QA, P16_KA, P16_QI, P16_KI = 0, 512, 1024, 1280
P16_WIDTH = 1408
PW_32 = P16_WIDTH
P32_VB, P32_RB, P32_UC, P32_WI, P32_QB, P32_KB, P32_GL = 0, 256, 512, 768, 896, 1024, 1152
P32_WIDTH = 1280

PROJ_TM = 512
DSA_TQ = 128
DSA_KC = 512
DSA_ACC = 32
BF16_ROWS = 16
VT_ROWS = LANES + BF16_ROWS
VT_WIDTH = (A_HEADS // 2) * VT_ROWS
LOG2E = 1.4426950408889634
DSA_FIRST_STEPS = 7
DSA_ROUND_STEPS = 2
DSA_RANK_MAX = 16
DSA_MAX_ROUNDS = 96
GLA_CT = 512
GLA_C = 64
GLA_UNROLL = 8
MIX_TM = 512
POOL_HALO = 16
FFN_TM = 512
FFN_GROUP = 768

NEG_BIG = -1e30
_NT = (((1,), (1,)), ((), ()))
_TN = (((0,), (0,)), ((), ()))


def _split2(x, keep_rest=False):
    hi = x.astype(jnp.bfloat16)
    rest = x - hi.astype(jnp.float32)
    return hi, (rest if keep_rest else rest.astype(jnp.bfloat16))


def _rms(x, g):
    ms = jnp.mean(x * x, axis=-1, keepdims=True)
    return x * lax.rsqrt(ms + EPS) * g


def _proj_kernel(x_ref, pos_ref, g_ref, w_ref, wvt_ref, tab_ref, o16_ref, vt_ref, o32_ref):
    h = _rms(x_ref[...], g_ref[...]).astype(jnp.bfloat16)
    ang = pos_ref[...].astype(jnp.float32) * tab_ref[0:1, :]
    cos = jnp.cos(ang)
    sin = jnp.sin(ang)
    sin_lo = sin * tab_ref[1:2, :]
    sin_hi = sin * tab_ref[2:3, :]

    def seg(a, b):
        return jnp.dot(h, w_ref[:, a:b], preferred_element_type=jnp.float32)

    def rope_store(p, col, scale=None):
        for g in range(p.shape[1] // LANES):
            xg = p[:, g * LANES:(g + 1) * LANES]
            half = ROT_DIM // 2
            r = xg * cos + pltpu.roll(xg, LANES - half, 1) * sin_lo + pltpu.roll(xg, half, 1) * sin_hi
            if scale is not None:
                r = r * scale
            o16_ref[:, col + g * LANES: col + (g + 1) * LANES] = r.astype(jnp.bfloat16)

    rope_store(seg(P16_QA, P16_KA), P16_QA, ATTN_SCALE * LOG2E)
    rope_store(seg(P16_KA, P16_QI), P16_KA)
    rope_store(seg(P16_QI, P16_KI), P16_QI)
    rope_store(seg(P16_KI, P16_WIDTH), P16_KI)
    vt = lax.dot_general(wvt_ref[...], h, _NT,
                         preferred_element_type=jnp.float32).astype(jnp.bfloat16)
    for p in range(A_HEADS // 2):
        vt_ref[0, p * VT_ROWS:p * VT_ROWS + LANES, :] = vt[p * LANES:(p + 1) * LANES, :]
        vt_ref[0, p * VT_ROWS + LANES:(p + 1) * VT_ROWS, :] = jnp.ones((BF16_ROWS, vt.shape[1]), jnp.bfloat16)
    base = PW_32
    o32_ref[:, P32_VB:P32_WI] = seg(base + P32_VB, base + P32_WI)
    o32_ref[:, P32_WI:P32_QB] = seg(base + P32_WI, base + P32_QB) * IDX_SCALE
    o32_ref[:, P32_QB:P32_WIDTH] = seg(base + P32_QB, base + P32_WIDTH)


def _proj(x2, pos2, g, w_pad, wvt, tab):
    T = x2.shape[0]
    tm = PROJ_TM
    return pl.pallas_call(
        _proj_kernel,
        grid=(T // tm,),
        in_specs=[
            pl.BlockSpec((tm, D_MODEL), lambda i: (i, 0)),
            pl.BlockSpec((tm, 1), lambda i: (i, 0)),
            pl.BlockSpec((1, D_MODEL), lambda i: (0, 0)),
            pl.BlockSpec((D_MODEL, PW_32 + P32_WIDTH), lambda i: (0, 0)),
            pl.BlockSpec((A_WIDTH, D_MODEL), lambda i: (0, 0)),
            pl.BlockSpec((SUBLANES, LANES), lambda i: (0, 0)),
        ],
        out_specs=[
            pl.BlockSpec((tm, P16_WIDTH), lambda i: (i, 0)),
            pl.BlockSpec((1, VT_WIDTH, tm), lambda i: (i, 0, 0)),
            pl.BlockSpec((tm, P32_WIDTH), lambda i: (i, 0)),
        ],
        out_shape=[
            jax.ShapeDtypeStruct((T, P16_WIDTH), jnp.bfloat16),
            jax.ShapeDtypeStruct((T // tm, VT_WIDTH, tm), jnp.bfloat16),
            jax.ShapeDtypeStruct((T, P32_WIDTH), jnp.float32),
        ],
        compiler_params=pltpu.CompilerParams(
            dimension_semantics=("arbitrary",), vmem_limit_bytes=VMEM_LIMIT),
        name="proj",
    )(x2, pos2, g, w_pad, wvt, tab)


def _dsa_kernel(q_ref, k_ref, vt_ref, qi_ref, ki_ref, wi_ref, o_ref,
                s_ref, lg_ref, cls_ref, qz_ref, m_ref, acc_ref, sb_ref, cm_ref, *, n_keep):
    TQ, KC = DSA_TQ, DSA_KC
    i = pl.program_id(1)
    q0 = i * TQ
    nkc = (q0 + TQ + KC - 1) // KC
    kf = jnp.float32(n_keep)

    lane = lax.broadcasted_iota(jnp.int32, (1, LANES), 1)
    first_half = lane < HEAD_DIM
    tpos = (q0 + lax.broadcasted_iota(jnp.int32, (1, TQ), 1)).astype(jnp.float32)
    kidx0 = lax.broadcasted_iota(jnp.int32, (KC, 1), 0).astype(jnp.float32)

    zero16 = jnp.zeros((), jnp.bfloat16)
    qi = qi_ref[...]
    qiz = jnp.concatenate(
        [jnp.where(first_half if h % 2 == 0 else ~first_half, qi[:, (h // 2) * LANES:(h // 2 + 1) * LANES], zero16)
         for h in range(IDX_HEADS)], axis=0)
    w_rows = wi_ref[...].T

    last_chunk = k_ref.shape[0] // KC - 1

    def score_logits(c, buf):
        c = jnp.asarray(c, jnp.int32)
        r0 = pl.multiple_of(jnp.minimum(c, last_chunk) * KC, KC)
        lg_ref[buf] = lax.dot_general(ki_ref[pl.ds(r0, KC), :], qiz, _NT,
                                      preferred_element_type=jnp.float32)

    def score_reduce(c, buf, carry):
        rmin, rmax = carry
        sc = w_rows[0:1, :] * jnp.maximum(lg_ref[buf, :, 0:TQ], 0.0)
        for h in range(1, IDX_HEADS):
            sc = sc + w_rows[h:h + 1, :] * jnp.maximum(lg_ref[buf, :, h * TQ:(h + 1) * TQ], 0.0)
        causal = (kidx0 + (c * KC).astype(jnp.float32)) <= tpos
        s_ref[c] = jnp.where(causal, sc, -jnp.inf)
        rmin = jnp.minimum(rmin, jnp.min(jnp.where(causal, sc, jnp.inf), axis=0, keepdims=True))
        rmax = jnp.maximum(rmax, jnp.max(jnp.where(causal, sc, -jnp.inf), axis=0, keepdims=True))
        return rmin, rmax

    def score_body(cc, carry):
        c = 2 * cc
        score_logits(c + 1, 1)
        carry = score_reduce(c, 0, carry)
        score_logits(c + 2, 0)
        return score_reduce(c + 1, 1, carry)

    score_logits(0, 0)
    rcarry = lax.fori_loop(
        0, nkc // 2, score_body,
        (jnp.full((1, TQ), jnp.inf, jnp.float32), jnp.full((1, TQ), -jnp.inf, jnp.float32)))
    rmin, rmax = lax.cond(nkc % 2 == 1, lambda cr: score_reduce(nkc - 1, 0, cr), lambda cr: cr, rcarry)

    def fold(m):
        return jnp.sum(m.reshape(KC // DSA_ACC, DSA_ACC, TQ), axis=0)

    def count_gt(x):
        def body(c, acc):
            return acc + fold(jnp.where(s_ref[c] > x, 1.0, 0.0))
        acc = lax.fori_loop(0, nkc, body, jnp.zeros((DSA_ACC, TQ), jnp.float32))
        return jnp.sum(acc, axis=0, keepdims=True)

    def count_eq(x):
        def body(c, acc):
            return acc + fold(jnp.where(s_ref[c] == x, 1.0, 0.0))
        acc = lax.fori_loop(0, nkc, body, jnp.zeros((DSA_ACC, TQ), jnp.float32))
        return jnp.sum(acc, axis=0, keepdims=True)

    def class_min_above(x):
        cls_ref[...] = jnp.full((KC, TQ), jnp.inf, jnp.float32)

        def body(c, carry):
            blk = s_ref[c]
            cls_ref[...] = jnp.minimum(cls_ref[...], jnp.where(blk > x, blk, jnp.inf))
            return carry
        lax.fori_loop(0, nkc, body, 0)

    def kth_class_min(rank, depth):
        first = jnp.min(cls_ref[...], axis=0, keepdims=True)

        def body(j, carry):
            prev, cand = carry
            pm = jnp.where(cls_ref[...] <= prev, jnp.inf, cls_ref[...])
            cls_ref[...] = pm
            nxt = jnp.min(pm, axis=0, keepdims=True)
            take = jnp.logical_and(rank >= j.astype(jnp.float32), nxt < jnp.inf)
            return nxt, jnp.where(take, nxt, cand)
        return lax.fori_loop(2, depth + 1, body, (first, first))[1]

    n_causal = tpos + 1.0
    all_kept = n_causal <= kf
    zeros_row = jnp.zeros((1, TQ), jnp.float32)
    log_target = jnp.log(kf + 0.5)

    def yval(cnt):
        return jnp.log(jnp.maximum(cnt, 0.5)) - log_target

    c0 = count_gt(zeros_row)
    e0 = count_eq(zeros_row)
    at_zero = jnp.logical_and(jnp.logical_and(c0 < kf, c0 + e0 >= kf), jnp.logical_not(all_kept))
    done0 = jnp.where(jnp.logical_or(all_kept, at_zero), 1.0, 0.0)
    thr0 = jnp.where(at_zero, 0.0, -jnp.inf)
    cgt0 = jnp.where(at_zero, c0, 0.0)
    ties0 = jnp.where(at_zero, e0, 0.0)
    above = c0 >= kf
    below = jnp.logical_and(jnp.logical_not(above), rmax > 0.0)
    lo0 = jnp.where(above, 0.0, -(2.0 * jnp.abs(rmin) + 1.0))
    clo0 = jnp.where(above, c0, n_causal)
    hi0 = jnp.where(below, 0.0, rmax)
    chi0 = jnp.where(below, c0 + e0, 0.0)

    def narrow(nst):
        lo, hi, clo, chi, ylo, yhi, side = nst
        frac = jnp.minimum(jnp.maximum(ylo / jnp.maximum(ylo - yhi, 1e-9), 0.0), 1.0)
        mid = jnp.minimum(jnp.maximum(lo + frac * (hi - lo), lo), hi)
        c = count_gt(mid)
        ge = c >= kf
        y = yval(c)
        yhi = jnp.where(jnp.logical_and(ge, side > 0.5), 0.5 * yhi, yhi)
        ylo = jnp.where(jnp.logical_and(jnp.logical_not(ge), side < -0.5), 0.5 * ylo, ylo)
        return (jnp.where(ge, mid, lo), jnp.where(ge, hi, mid), jnp.where(ge, c, clo), jnp.where(ge, chi, c),
                jnp.where(ge, y, ylo), jnp.where(ge, yhi, y), jnp.where(ge, 1.0, -1.0))

    nst = (lo0, hi0, clo0, chi0, yval(clo0), yval(chi0), zeros_row)
    nst = lax.fori_loop(0, DSA_FIRST_STEPS - DSA_ROUND_STEPS, lambda _, s: narrow(s), nst)

    def round_cond(st):
        return jnp.logical_and(st[0] < DSA_MAX_ROUNDS, st[1] > 0.0)

    def round_body(st):
        it, _, nst, thr, cgt, done = st
        for _ in range(DSA_ROUND_STEPS):
            nst = narrow(nst)
        lo, hi, clo, chi, ylo, yhi, side = nst
        class_min_above(lo)
        rank = clo - kf + 1.0
        depth = jnp.max(jnp.where(done < 0.5, jnp.minimum(rank, float(DSA_RANK_MAX)), 1.0)).astype(jnp.int32)
        cand = kth_class_min(rank, depth)
        c2 = count_gt(cand)
        ok = c2 < kf
        newly = jnp.logical_and(ok, done < 0.5)
        thr = jnp.where(newly, cand, thr)
        cgt = jnp.where(newly, c2, cgt)
        done = jnp.where(ok, 1.0, done)
        up = jnp.logical_not(ok)
        nst = (jnp.where(up, cand, lo), hi, jnp.where(up, c2, clo), chi, jnp.where(up, yval(c2), ylo), yhi, side)
        return (it + 1, jnp.sum(1.0 - done), nst, thr, cgt, done)

    def check_cond(ost):
        return jnp.logical_and(ost[0] < DSA_MAX_ROUNDS, ost[1] > 0.0)

    def check_body(ost):
        _, _, it, nst, thr, cgt, ties, done, sure = ost
        it, _, nst, thr, cgt, done = lax.while_loop(
            round_cond, round_body, (it, jnp.sum(1.0 - done), nst, thr, cgt, done))
        e = count_eq(thr)
        good = jnp.logical_or(sure > 0.5, cgt + e >= kf)
        ties = jnp.where(sure > 0.5, ties, e)
        lo, hi, clo, chi, ylo, yhi, side = nst
        nst = (lo, jnp.where(good, hi, thr), clo, jnp.where(good, chi, cgt + e), ylo,
               jnp.where(good, yhi, yval(cgt + e)), side)
        sure = jnp.where(good, 1.0, 0.0)
        return (ost[0] + 1, jnp.sum(1.0 - sure), it, nst, thr, cgt, ties, sure, sure)

    ost = lax.while_loop(check_cond, check_body,
                         (jnp.int32(0), jnp.sum(1.0 - done0), jnp.int32(0), nst, thr0, cgt0, ties0, done0, done0))
    thr, cgt, ties = ost[4], ost[5], ost[6]

    need = kf - cgt
    excess = jnp.where(jnp.logical_and(ties > need, jnp.logical_not(all_kept)), 1.0, 0.0)
    jall = jnp.where(all_kept, -1.0, jnp.float32(1e9))

    def tie_search(_):
        G = LANES
        incl = (lax.broadcasted_iota(jnp.int32, (G, G), 0) >= lax.broadcasted_iota(jnp.int32, (G, G), 1))
        tri = jnp.where(incl, 1.0, 0.0).astype(jnp.bfloat16)

        def body(c, carry):
            run, jlast = carry
            for g in range(KC // G):
                tied = s_ref[c, g * G:(g + 1) * G, :] == thr
                pref = jnp.dot(tri, jnp.where(tied, 1.0, 0.0).astype(jnp.bfloat16),
                               preferred_element_type=jnp.float32)
                kidx = kidx0[g * G:(g + 1) * G, :] + (c * KC).astype(jnp.float32)
                kept = jnp.where(tied, jnp.where(pref + run <= need, kidx, -1.0), -1.0)
                run = run + pref[G - 1:G, :]
                jlast = jnp.maximum(jlast, jnp.max(kept, axis=0, keepdims=True))
            return run, jlast

        _, jlast = lax.fori_loop(0, nkc, body, (zeros_row, jnp.full((1, TQ), -1.0, jnp.float32)))
        return jnp.where(excess > 0.5, jlast, jall)

    jstar = lax.cond(jnp.sum(excess) > 0.0, tie_search, lambda _: jall, 0)

    zq = jnp.zeros((), q_ref.dtype)
    for p in range(A_HEADS // 2):
        qp = q_ref[:, p * LANES:(p + 1) * LANES]
        qz_ref[p, 0:TQ, :] = jnp.where(first_half, qp, zq)
        qz_ref[p, TQ:2 * TQ, :] = jnp.where(first_half, zq, qp)
    m_ref[...] = jnp.full(m_ref.shape, NEG_BIG, jnp.float32)
    acc_ref[...] = jnp.zeros(acc_ref.shape, jnp.float32)

    s_ref[nkc] = jnp.full((KC, TQ), -jnp.inf, jnp.float32)

    def stage_a(c, buf):
        c = jnp.asarray(c, jnp.int32)
        r0 = pl.multiple_of(jnp.minimum(c, last_chunk) * KC, KC)
        sc = s_ref[c]
        kidx = kidx0 + (c * KC).astype(jnp.float32)
        bias = jnp.where(sc > thr, 0.0,
                         jnp.where(sc == thr, jnp.where(kidx <= jstar, 0.0, NEG_BIG), NEG_BIG))
        bias2 = jnp.concatenate([bias, bias], axis=1)

        def pair(p):
            kc = k_ref[pl.ds(r0, KC), p * LANES:(p + 1) * LANES]
            s = lax.dot_general(kc, qz_ref[p], _NT, preferred_element_type=jnp.float32) + bias2
            sb_ref[buf, p] = s
            cm_ref[buf, p] = jnp.max(s, axis=0, keepdims=True)
        return pair

    def stage_b(c, buf):
        cv = jnp.minimum(c, last_chunk)

        def pair(p):
            vtc = vt_ref[cv, p * VT_ROWS:(p + 1) * VT_ROWS, :]
            m_old = m_ref[p]
            m_new = jnp.maximum(m_old, cm_ref[buf, p])
            alpha = jnp.exp2(m_old - m_new)
            pe = jnp.exp2((sb_ref[buf, p] - m_new).astype(jnp.bfloat16))
            acc_ref[p] = alpha * acc_ref[p] + jnp.dot(vtc, pe, preferred_element_type=jnp.float32)
            m_ref[p] = m_new
        return pair

    def run(stage):
        for p in range(A_HEADS // 2):
            stage(p)

    def interleave(a, b):
        for p in range(A_HEADS // 2):
            a(p)
            b(p)

    run(stage_a(0, 0))

    def attn_body(cc, carry):
        c = 2 * cc
        interleave(stage_a(c + 1, 1), stage_b(c, 0))
        interleave(stage_a(c + 2, 0), stage_b(c + 1, 1))
        return carry

    lax.fori_loop(0, nkc // 2, attn_body, 0)

    @pl.when(nkc % 2 == 1)
    def _():
        run(stage_b(nkc - 1, 0))

    top_rows = lax.broadcasted_iota(jnp.int32, (LANES, 1), 0) < HEAD_DIM
    for p in range(A_HEADS // 2):
        acc = acc_ref[p]
        o = acc[0:LANES] / acc[LANES:LANES + 1]
        o_t = jnp.where(top_rows, o[:, 0:TQ], o[:, TQ:2 * TQ])
        o_ref[:, p * LANES:(p + 1) * LANES] = o_t.T.astype(o_ref.dtype)


def _dsa(p16, vt, p32, nbat, S):
    T = nbat * S
    TQ, KC = DSA_TQ, DSA_KC
    nq = S // TQ
    nc = S // KC
    n_keep = min(TOPK_MAX, S // 4)
    once = pl.Buffered(1)
    return pl.pallas_call(
        functools.partial(_dsa_kernel, n_keep=n_keep),
        grid=(nbat, nq),
        in_specs=[
            pl.BlockSpec((TQ, A_WIDTH), lambda b, i: (b * nq + i, P16_QA // A_WIDTH)),
            pl.BlockSpec((S, A_WIDTH), lambda b, i: (b, P16_KA // A_WIDTH), pipeline_mode=once),
            pl.BlockSpec((nc, VT_WIDTH, KC), lambda b, i: (b, 0, 0), pipeline_mode=once),
            pl.BlockSpec((TQ, 2 * LANES), lambda b, i: (b * nq + i, P16_QI // (2 * LANES))),
            pl.BlockSpec((S, LANES), lambda b, i: (b, P16_KI // LANES), pipeline_mode=once),
            pl.BlockSpec((TQ, LANES), lambda b, i: (b * nq + i, P32_WI // LANES)),
        ],
        out_specs=pl.BlockSpec((TQ, A_WIDTH), lambda b, i: (b * nq + i, 0)),
        out_shape=jax.ShapeDtypeStruct((T, A_WIDTH), jnp.bfloat16),
        scratch_shapes=[
            pltpu.VMEM((nc + 1, KC, TQ), jnp.float32),
            pltpu.VMEM((2, KC, IDX_HEADS * TQ), jnp.float32),
            pltpu.VMEM((KC, TQ), jnp.float32),
            pltpu.VMEM((A_HEADS // 2, 2 * TQ, LANES), jnp.bfloat16),
            pltpu.VMEM((A_HEADS // 2, 1, 2 * TQ), jnp.float32),
            pltpu.VMEM((A_HEADS // 2, VT_ROWS, 2 * TQ), jnp.float32),
            pltpu.VMEM((2, A_HEADS // 2, KC, 2 * TQ), jnp.float32),
            pltpu.VMEM((2, A_HEADS // 2, 1, 2 * TQ), jnp.float32),
        ],
        compiler_params=pltpu.CompilerParams(
            dimension_semantics=("arbitrary", "arbitrary"), vmem_limit_bytes=VMEM_LIMIT),
        name="dsa",
    )(p16, p16, vt, p16, p16, p32)


def _gla_kernel(q_ref, k_ref, v_ref, r_ref, gl_ref, wg_ref, bg_ref, gn_ref, o_ref, st_ref):
    C = GLA_C

    @pl.when(pl.program_id(1) == 0)
    def _():
        st_ref[...] = jnp.zeros(st_ref.shape, jnp.float32)

    row = lax.broadcasted_iota(jnp.int32, (C, C), 0)
    col = lax.broadcasted_iota(jnp.int32, (C, C), 1)
    tril = row >= col
    tril_b = jnp.where(tril, 1.0, 0.0).astype(jnp.bfloat16)
    tril4 = jnp.concatenate([tril] * B_HEADS, axis=0)
    khead = lax.broadcasted_iota(jnp.int32, (1, B_KWIDTH), 1) // B_DK
    vhead = lax.broadcasted_iota(jnp.int32, (1, B_VWIDTH), 1) // B_DV
    st_rows = lax.broadcasted_iota(jnp.int32, (B_VWIDTH, B_KWIDTH), 0) // B_DV
    st_cols = lax.broadcasted_iota(jnp.int32, (B_VWIDTH, B_KWIDTH), 1) // B_DK
    same_head = st_rows == st_cols
    seg_r = lax.broadcasted_iota(jnp.int32, (B_VWIDTH, B_VWIDTH), 0) // B_DV
    seg_c = lax.broadcasted_iota(jnp.int32, (B_VWIDTH, B_VWIDTH), 1) // B_DV
    seg_mean = jnp.where(seg_r == seg_c, 1.0 / B_DV, 0.0).astype(jnp.bfloat16)
    wg_hi, wg_lo = _split2(wg_ref[...])
    bg = bg_ref[...]
    gn = gn_ref[...]

    def chunk(ci, st):
        r0 = pl.multiple_of(ci * C, C)
        q = q_ref[pl.ds(r0, C), :]
        k = k_ref[pl.ds(r0, C), :]
        v = v_ref[pl.ds(r0, C), :].astype(jnp.bfloat16)
        g_hi, g_lo = _split2(gl_ref[pl.ds(r0, C), :])
        zz = jnp.dot(jnp.concatenate([g_hi, g_lo], axis=0), wg_hi, preferred_element_type=jnp.float32)
        z = zz[0:C] + zz[C:2 * C] + jnp.dot(g_hi, wg_lo, preferred_element_type=jnp.float32) + bg
        log_a = (jnp.minimum(z, 0.0) - jnp.log1p(jnp.exp(-jnp.abs(z)))) / GATE_TAU
        a_hi, a_rest = _split2(log_a, keep_rest=True)
        a_mid, a_lo = _split2(a_rest)
        bb = jnp.dot(tril_b, jnp.concatenate([a_hi, a_mid, a_lo], axis=1), preferred_element_type=jnp.float32)
        b = bb[:, 0:B_KWIDTH] + bb[:, B_KWIDTH:2 * B_KWIDTH] + bb[:, 2 * B_KWIDTH:3 * B_KWIDTH]
        b_last = b[C - 1:C, :]
        qt = (q * (B_DK ** -0.5)) * jnp.exp(b)
        kt = (k * jnp.exp(-b)).astype(jnp.bfloat16)
        kh = (k * jnp.exp(b_last - b)).astype(jnp.bfloat16)
        qz = jnp.concatenate([jnp.where(khead == h, qt, 0.0) for h in range(B_HEADS)], axis=0)
        att = lax.dot_general(qz.astype(jnp.bfloat16), kt, _NT, preferred_element_type=jnp.float32)
        att = jnp.where(tril4, att, 0.0).astype(jnp.bfloat16)
        res = jnp.dot(att, v, preferred_element_type=jnp.float32)
        o = lax.dot_general(qt.astype(jnp.bfloat16), st.astype(jnp.bfloat16), _NT,
                            preferred_element_type=jnp.float32)
        for h in range(B_HEADS):
            o = o + jnp.where(vhead == h, res[h * C:(h + 1) * C], 0.0)
        upd = lax.dot_general(v, kh, _TN, preferred_element_type=jnp.float32)
        st_new = st * jnp.exp(b_last) + jnp.where(same_head, upd, 0.0)
        o2_hi, o2_lo = _split2(o * o)
        mm = jnp.dot(jnp.concatenate([o2_hi, o2_lo], axis=0), seg_mean, preferred_element_type=jnp.float32)
        ms = mm[0:C] + mm[C:2 * C]
        y = o * lax.rsqrt(ms + EPS) * gn
        r = r_ref[pl.ds(r0, C), :]
        o_ref[pl.ds(r0, C), :] = (y * (r * jax.nn.sigmoid(r))).astype(o_ref.dtype)
        return st_new

    st_ref[...] = lax.fori_loop(0, GLA_CT // C, chunk, st_ref[...], unroll=GLA_UNROLL)


def _gla(p32, wg_pad, bg, gn, nbat, S):
    T = nbat * S
    CT = GLA_CT
    nt = S // CT
    row = lambda b, i: b * nt + i
    return pl.pallas_call(
        _gla_kernel,
        grid=(nbat, nt),
        in_specs=[
            pl.BlockSpec((CT, B_KWIDTH), lambda b, i: (row(b, i), P32_QB // B_KWIDTH)),
            pl.BlockSpec((CT, B_KWIDTH), lambda b, i: (row(b, i), P32_KB // B_KWIDTH)),
            pl.BlockSpec((CT, B_VWIDTH), lambda b, i: (row(b, i), P32_VB // B_VWIDTH)),
            pl.BlockSpec((CT, B_VWIDTH), lambda b, i: (row(b, i), P32_RB // B_VWIDTH)),
            pl.BlockSpec((CT, LANES), lambda b, i: (row(b, i), P32_GL // LANES)),
            pl.BlockSpec((LANES, B_KWIDTH), lambda b, i: (0, 0)),
            pl.BlockSpec((1, B_KWIDTH), lambda b, i: (0, 0)),
            pl.BlockSpec((1, B_VWIDTH), lambda b, i: (0, 0)),
        ],
        out_specs=pl.BlockSpec((CT, B_VWIDTH), lambda b, i: (row(b, i), 0)),
        out_shape=jax.ShapeDtypeStruct((T, B_VWIDTH), jnp.bfloat16),
        scratch_shapes=[pltpu.VMEM((B_VWIDTH, B_KWIDTH), jnp.float32)],
        compiler_params=pltpu.CompilerParams(
            dimension_semantics=("arbitrary", "arbitrary"), vmem_limit_bytes=VMEM_LIMIT),
        name="gla",
    )(p32, p32, p32, p32, p32, wg_pad, bg, gn)


def _mix_kernel(x_ref, oa_ref, ob_ref, uc_ref, uh_ref, wo_ref, wp_ref, ps_ref, o_ref, ext_ref, *, seq):
    TM = MIX_TM
    t0 = (pl.program_id(0) * TM) % seq
    uc = uc_ref[...]
    @pl.when(t0 == 0)
    def _():
        ext_ref[0:POOL_HALO, :] = jnp.zeros((POOL_HALO, C_WIDTH), jnp.float32)

    @pl.when(t0 != 0)
    def _():
        ext_ref[0:POOL_HALO, :] = uh_ref[...]

    ext_ref[POOL_HALO:POOL_HALO + TM, :] = uc
    lane = lax.broadcasted_iota(jnp.int32, (1, C_WIDTH), 1)
    group = lane // C_GROUP_DIM
    wsum = jnp.zeros_like(uc)
    cum = uc
    for j in range(1, max(POOL_WINDOWS)):
        cum = cum + ext_ref[POOL_HALO - j:POOL_HALO - j + TM, :]
        if (j + 1) in POOL_WINDOWS:
            wsum = jnp.where(group == POOL_WINDOWS.index(j + 1), cum, wsum)
    win = jnp.zeros((1, C_WIDTH), jnp.float32)
    for g, w in enumerate(POOL_WINDOWS):
        win = jnp.where(group == g, float(w), win)
    tpos = (t0 + lax.broadcasted_iota(jnp.int32, (TM, 1), 0)).astype(jnp.float32)
    cnt = jnp.minimum(tpos + 1.0, win)
    pooled = wsum / cnt - uc
    y = jnp.dot(pooled.astype(jnp.bfloat16), wp_ref[...], preferred_element_type=jnp.float32) * ps_ref[...]
    mix = jnp.dot(oa_ref[...], wo_ref[0:A_WIDTH, :], preferred_element_type=jnp.float32)
    mix = mix + jnp.dot(ob_ref[...], wo_ref[A_WIDTH:A_WIDTH + B_VWIDTH, :], preferred_element_type=jnp.float32)
    mix = mix + jnp.dot(y.astype(jnp.bfloat16), wo_ref[A_WIDTH + B_VWIDTH:MIX_WIDTH, :],
                        preferred_element_type=jnp.float32)
    o_ref[...] = x_ref[...] + mix


def _mix(x2, o_a, o_b, p32, w_out, wp_bd, ps, S):
    T = x2.shape[0]
    TM = MIX_TM
    per = TM // POOL_HALO
    return pl.pallas_call(
        functools.partial(_mix_kernel, seq=S),
        grid=(T // TM,),
        in_specs=[
            pl.BlockSpec((TM, D_MODEL), lambda i: (i, 0)),
            pl.BlockSpec((TM, A_WIDTH), lambda i: (i, 0)),
            pl.BlockSpec((TM, B_VWIDTH), lambda i: (i, 0)),
            pl.BlockSpec((TM, C_WIDTH), lambda i: (i, P32_UC // C_WIDTH)),
            pl.BlockSpec((POOL_HALO, C_WIDTH), lambda i: (jnp.maximum(i * per - 1, 0), P32_UC // C_WIDTH)),
            pl.BlockSpec((MIX_WIDTH, D_MODEL), lambda i: (0, 0)),
            pl.BlockSpec((C_WIDTH, C_WIDTH), lambda i: (0, 0)),
            pl.BlockSpec((1, C_WIDTH), lambda i: (0, 0)),
        ],
        out_specs=pl.BlockSpec((TM, D_MODEL), lambda i: (i, 0)),
        out_shape=jax.ShapeDtypeStruct((T, D_MODEL), jnp.float32),
        scratch_shapes=[pltpu.VMEM((POOL_HALO + TM, C_WIDTH), jnp.float32)],
        compiler_params=pltpu.CompilerParams(
            dimension_semantics=("arbitrary",), vmem_limit_bytes=VMEM_LIMIT),
        name="mix",
    )(x2, o_a, o_b, p32, p32, w_out, wp_bd, ps)


def _ffn_kernel(x_ref, g_ref, wu_ref, cw_ref, cb_ref, wd_ref, fg_ref, o_ref, ext_ref, carry_ref, *, seq, final):
    TM = FFN_TM
    t0 = (pl.program_id(0) * TM) % seq

    @pl.when(t0 == 0)
    def _():
        carry_ref[...] = jnp.zeros(carry_ref.shape, jnp.float32)

    x = x_ref[...]
    h = _rms(x, g_ref[...]).astype(jnp.bfloat16)

    def up(lo, hi):
        a = jnp.dot(h, wu_ref[:, lo:hi], preferred_element_type=jnp.float32)
        b = jnp.dot(h, wu_ref[:, D_FF + lo:D_FF + hi], preferred_element_type=jnp.float32)
        return a, b

    def gated(k, lo, hi, a, b):
        w = hi - lo
        buf = k % 2
        ext_ref[buf, 0:SUBLANES, 0:w] = carry_ref[:, lo:hi]
        ext_ref[buf, SUBLANES:SUBLANES + TM, 0:w] = a
        carry_ref[:, lo:hi] = a[TM - SUBLANES:TM, :]
        conv = cb_ref[:, lo:hi] + a * cw_ref[CONV_WIDTH - 1:CONV_WIDTH, lo:hi]
        for d in range(1, CONV_WIDTH):
            conv = conv + (ext_ref[buf, SUBLANES - d:SUBLANES - d + TM, 0:w]
                           * cw_ref[CONV_WIDTH - 1 - d:CONV_WIDTH - d, lo:hi])
        return (conv * jax.nn.sigmoid(conv) * b).astype(jnp.bfloat16)

    groups = [(lo, min(lo + FFN_GROUP, D_FF)) for lo in range(0, D_FF, FFN_GROUP)]
    ab = up(*groups[0])
    y = x
    for k, (lo, hi) in enumerate(groups):
        nxt = up(*groups[k + 1]) if k + 1 < len(groups) else None
        y = y + jnp.dot(gated(k, lo, hi, *ab), wd_ref[lo:hi, :], preferred_element_type=jnp.float32)
        ab = nxt
    if final:
        y = _rms(y, fg_ref[...])
    o_ref[...] = y


def _ffn(x2, g, w_up, conv_w, conv_b, w_down, fg, S, final):
    T = x2.shape[0]
    TM = FFN_TM
    once = pl.Buffered(1)
    return pl.pallas_call(
        functools.partial(_ffn_kernel, seq=S, final=final),
        grid=(T // TM,),
        in_specs=[
            pl.BlockSpec((TM, D_MODEL), lambda i: (i, 0)),
            pl.BlockSpec((1, D_MODEL), lambda i: (0, 0)),
            pl.BlockSpec((D_MODEL, 2 * D_FF), lambda i: (0, 0), pipeline_mode=once),
            pl.BlockSpec((CONV_WIDTH, D_FF), lambda i: (0, 0)),
            pl.BlockSpec((1, D_FF), lambda i: (0, 0)),
            pl.BlockSpec((D_FF, D_MODEL), lambda i: (0, 0), pipeline_mode=once),
            pl.BlockSpec((1, D_MODEL), lambda i: (0, 0)),
        ],
        out_specs=pl.BlockSpec((TM, D_MODEL), lambda i: (i, 0)),
        out_shape=jax.ShapeDtypeStruct((T, D_MODEL), jnp.float32),
        scratch_shapes=[
            pltpu.VMEM((2, SUBLANES + TM, FFN_GROUP), jnp.float32),
            pltpu.VMEM((SUBLANES, D_FF), jnp.float32),
        ],
        compiler_params=pltpu.CompilerParams(
            dimension_semantics=("arbitrary",), vmem_limit_bytes=VMEM_LIMIT),
        name="ffn_final" if final else "ffn",
    )(x2, g, w_up, conv_w, conv_b, w_down, fg)


def _pad_cols(w, width):
    return jnp.pad(w, ((0, 0), (0, width - w.shape[1])))


def _pack_w_in(w):
    sizes = (A_WIDTH, A_WIDTH, A_WIDTH, IDX_HEADS * IDX_DIM, IDX_DIM, IDX_HEADS,
             B_KWIDTH, B_KWIDTH, B_VWIDTH, B_VWIDTH, GATE_RANK, C_WIDTH)
    parts, off = [], 0
    for n in sizes:
        parts.append(w[:, off:off + n])
        off += n
    qa, ka, va, qi, ki, wi, qb, kb, vb, rb, gl, uc = parts
    cols = [qa, ka, qi, ki, ki, vb, rb, uc, _pad_cols(wi, LANES), qb, kb, _pad_cols(gl, LANES)]
    return jnp.concatenate(cols, axis=1).astype(jnp.bfloat16), va.T.astype(jnp.bfloat16)


def _rope_table():
    lane = jnp.arange(LANES) % HEAD_DIM
    inv = ROPE_THETA ** (-jnp.arange(0, ROT_DIM, 2, dtype=jnp.float32) / ROT_DIM)
    half = ROT_DIM // 2
    inv_lane = jnp.where(lane < ROT_DIM, inv[lane % half], 0.0)
    lo = jnp.where(lane < half, -1.0, 0.0)
    hi = jnp.where((lane >= half) & (lane < ROT_DIM), 1.0, 0.0)
    tab = jnp.zeros((SUBLANES, LANES), jnp.float32)
    return tab.at[0].set(inv_lane).at[1].set(lo).at[2].set(hi)


def kernel(x, positions, norm1_g, w_in, w_gate_up, b_gate, gla_norm_g, w_pool, pool_scale, w_out, norm2_g, w_up, conv_w, conv_b, w_down, final_norm_g):
    nbat, S, _ = x.shape
    T = nbat * S
    depth = w_in.shape[0]
    assert S % DSA_KC == 0 and S % GLA_CT == 0 and S % MIX_TM == 0 and S % FFN_TM == 0 and PROJ_TM == DSA_KC
    x2 = x.reshape(T, D_MODEL)
    pos2 = positions.reshape(T, 1)
    tab = _rope_table()
    fg = final_norm_g.reshape(1, D_MODEL)
    bf = jnp.bfloat16
    for l in range(depth):
        w_pad, wvt = _pack_w_in(w_in[l])
        p16, vt, p32 = _proj(x2, pos2, norm1_g[l].reshape(1, D_MODEL), w_pad, wvt, tab)
        o_a = _dsa(p16, vt, p32, nbat, S)
        wg_pad = jnp.pad(w_gate_up[l], ((0, LANES - GATE_RANK), (0, 0)))
        o_b = _gla(p32, wg_pad, b_gate[l].reshape(1, B_KWIDTH), gla_norm_g[l].reshape(1, B_VWIDTH), nbat, S)
        wp_bd = jax.scipy.linalg.block_diag(*[w_pool[l, g] for g in range(C_GROUPS)]).astype(bf)
        x2 = _mix(x2, o_a, o_b, p32, w_out[l].astype(bf), wp_bd, pool_scale[l].reshape(1, C_WIDTH), S)
        x2 = _ffn(x2, norm2_g[l].reshape(1, D_MODEL), w_up[l].astype(bf), conv_w[l],
                  conv_b[l].reshape(1, D_FF), w_down[l].astype(bf), fg, S, final=(l == depth - 1))
    return x2.reshape(nbat, S, D_MODEL)
```
